```python
import jax, jax.numpy as jnp
from jax import lax
import numpy as np

D_MODEL = 4096
BATCH = 2
SEQ = 4096
DEPTH = 2
DEC_BATCH = 8
DEC_SEQ = 16
PAST_LEN = 1024

CHUNK = 64
EPS = 1e-6
D_CONV = D_MODEL // 4
CONV_WIDTH = 31
HGRN_HEADS = 16
HGRN_DK = 128
HGRN_DV = 128
D_HGRN_K = HGRN_HEADS * HGRN_DK
D_HGRN_V = HGRN_HEADS * HGRN_DV
SCAN_BLOCK = 16
N_MEM = 256
MEM_HEADS = 4
MEM_HEAD_DIM = D_MODEL // 16
D_MEM = MEM_HEADS * MEM_HEAD_DIM
MEM_SCALE = MEM_HEAD_DIM ** -0.5
N_BRANCH = 3
N_GROUPS = 4
EXPERTS_PER_GROUP = 8
N_EXPERTS = N_GROUPS * EXPERTS_PER_GROUP
TOP_K_IN_GROUP = 2
D_EXPERT = D_MODEL // 4
MOE_BLOCK = 128
SPLITS = (D_CONV, 2 * D_CONV, 2 * D_CONV + D_HGRN_K, 2 * D_CONV + D_HGRN_K + D_HGRN_V,
          2 * D_CONV + 2 * D_HGRN_K + D_HGRN_V, 2 * D_CONV + 2 * D_HGRN_K + 2 * D_HGRN_V,
          2 * D_CONV + 2 * D_HGRN_K + 2 * D_HGRN_V + D_MEM)
D_IN = SPLITS[-1] + N_BRANCH * D_MODEL

kernel_name = 'hybrid_stream_conv_hgrn2_hmoe_step'


def rmsnorm(x, g):
    xf = x.astype(jnp.float32)
    y = xf * lax.rsqrt(jnp.mean(xf * xf, axis=-1, keepdims=True) + EPS)
    return (y * g.astype(jnp.float32)).astype(x.dtype)


def layernorm(x, g, b):
    xf = x.astype(jnp.float32)
    mu = jnp.mean(xf, axis=-1, keepdims=True)
    var = jnp.mean(jnp.square(xf - mu), axis=-1, keepdims=True)
    return ((xf - mu) * lax.rsqrt(var + EPS) * g.astype(jnp.float32) + b.astype(jnp.float32)).astype(x.dtype)


def hgrn_block(S, xs):
    q, k, v, log_f = xs
    L = q.shape[1]
    b = jnp.cumsum(log_f, axis=1)
    causal = jnp.tril(jnp.ones((L, L), dtype=bool))[None, :, :, None, None]
    decay = jnp.exp(jnp.where(causal, b[:, :, None] - b[:, None, :], -jnp.inf))
    scores = jnp.einsum('bthd,bshd,btshd->bhts', q, k, decay)
    o = jnp.einsum('bhts,bshv->bthv', scores, v) + jnp.einsum('bthd,bhdv->bthv', q * jnp.exp(b), S)
    b_last = b[:, -1:]
    S_new = jnp.exp(b_last[:, 0])[..., None] * S + jnp.einsum('bshd,bshv->bhdv', k * jnp.exp(b_last - b), v)
    return S_new, o


def hgrn_scan(S0, q, k, v, log_f):
    B, L = q.shape[0], q.shape[1]
    blk = SCAN_BLOCK if L % SCAN_BLOCK == 0 else L
    nb = L // blk

    def to_blocks(a):
        return jnp.moveaxis(a.reshape(B, nb, blk, *a.shape[2:]), 1, 0)

    S_fin, o = lax.scan(hgrn_block, S0, (to_blocks(q), to_blocks(k), to_blocks(v), to_blocks(log_f)))
    return S_fin, jnp.moveaxis(o, 0, 1).reshape(B, L, HGRN_HEADS, HGRN_DV)


def hier_moe(h, w_rg, b_rg, w_re, b_re, w_ein, w_eout):
    B, L, D = h.shape
    xf = h.reshape(-1, D)
    T = xf.shape[0]
    g_logits = (xf @ w_rg).astype(jnp.float32) + b_rg.astype(jnp.float32)
    g_prob = jax.nn.softmax(g_logits, axis=-1)
    grp = jnp.argmax(g_logits, axis=-1).astype(jnp.int32)
    p_grp = jnp.take_along_axis(g_prob, grp[:, None], axis=-1)
    e_logits = ((xf @ w_re).astype(jnp.float32) + b_re.astype(jnp.float32)).reshape(T, N_GROUPS, EXPERTS_PER_GROUP)
    e_sel = jnp.take_along_axis(e_logits, grp[:, None, None], axis=1)[:, 0]
    top_val, top_idx = lax.top_k(e_sel, TOP_K_IN_GROUP)
    gate = p_grp * jax.nn.softmax(top_val, axis=-1)
    eid = grp[:, None] * EXPERTS_PER_GROUP + top_idx
    A = T * TOP_K_IN_GROUP
    flat_e = eid.reshape(-1)
    flat_tok = jnp.repeat(jnp.arange(T, dtype=jnp.int32), TOP_K_IN_GROUP)
    flat_gate = gate.reshape(-1)
    order = jnp.argsort(flat_e)
    sorted_e = flat_e[order]
    counts = jnp.zeros((N_EXPERTS,), jnp.int32).at[flat_e].add(1)
    padded = (counts + MOE_BLOCK - 1) // MOE_BLOCK * MOE_BLOCK
    start = jnp.cumsum(counts) - counts
    pend = jnp.cumsum(padded)
    pstart = pend - padded
    dest = pstart[sorted_e] + jnp.arange(A, dtype=jnp.int32) - start[sorted_e]
    n_blocks = -(-(A + N_EXPERTS * (MOE_BLOCK - 1)) // MOE_BLOCK)
    P = n_blocks * MOE_BLOCK
    buf_tok = jnp.full((P,), T, jnp.int32).at[dest].set(flat_tok[order])
    buf_gate = jnp.zeros((P,), jnp.float32).at[dest].set(flat_gate[order])
    blk_start = jnp.arange(n_blocks, dtype=jnp.int32) * MOE_BLOCK
    blk_e = jnp.minimum(jnp.sum(blk_start[:, None] >= pend[None, :], axis=1), N_EXPERTS - 1)
    xpad = jnp.concatenate([xf, jnp.zeros((1, D), xf.dtype)], axis=0)
    xb = xpad[buf_tok].reshape(n_blocks, MOE_BLOCK, D)

    def expert_block(args):
        xblk, e = args
        a, u = jnp.split(xblk @ w_ein[e], 2, axis=-1)
        return (jax.nn.silu(a) * u) @ w_eout[e]

    yb = lax.map(expert_block, (xb, blk_e)).reshape(P, D)
    yb = yb * buf_gate[:, None].astype(yb.dtype)
    out = jnp.zeros((T + 1, D), yb.dtype).at[buf_tok].add(yb)
    return out[:T].reshape(B, L, D).astype(h.dtype)


def layer(x, conv_state, S0, mem_k, mem_v, lb, p):
    B, L, _ = x.shape
    h = rmsnorm(x, p['norm_mix'])
    proj = h @ p['w_in']
    glu_a, glu_b, f_raw, i_raw, q_raw, g_raw, q_mem, gate_raw = jnp.split(proj, SPLITS, axis=-1)
    u = glu_a * jax.nn.sigmoid(glu_b)
    u_pad = jnp.concatenate([conv_state.astype(u.dtype), u], axis=1)
    c = lax.conv_general_dilated(u_pad, p['w_dw'][:, None, :].astype(u.dtype), (1,), 'VALID',
                                 dimension_numbers=('NWC', 'WIO', 'NWC'), feature_group_count=D_CONV)
    c = jax.nn.silu(layernorm(c + p['b_dw'], p['conv_ln_g'], p['conv_ln_b']))
    y_conv = c @ p['w_conv_out']
    new_conv_state = u_pad[:, L:]
    shp_k = (B, L, HGRN_HEADS, HGRN_DK)
    shp_v = (B, L, HGRN_HEADS, HGRN_DV)
    lb_h = lb.reshape(HGRN_HEADS, HGRN_DK)
    log_f = jnp.logaddexp(jnp.log(lb_h), jnp.log1p(-lb_h) + jax.nn.log_sigmoid(f_raw.reshape(shp_k).astype(jnp.float32)))
    k_in = -jnp.expm1(log_f)
    S_new, o = hgrn_scan(S0.astype(jnp.float32), q_raw.reshape(shp_k).astype(jnp.float32), k_in,
                         i_raw.reshape(shp_v).astype(jnp.float32), log_f)
    o = rmsnorm(o, p['hgrn_norm'].reshape(HGRN_HEADS, HGRN_DV)) * jax.nn.silu(g_raw.reshape(shp_v).astype(jnp.float32))
    y_hgrn = o.reshape(B, L, D_HGRN_V).astype(x.dtype) @ p['w_hgrn_out']
    qm = q_mem.reshape(B, L, MEM_HEADS, MEM_HEAD_DIM)
    s = jnp.einsum('blhd,bnhd->bhln', qm, mem_k.astype(qm.dtype)).astype(jnp.float32) * MEM_SCALE
    pr = jax.nn.softmax(s, axis=-1).astype(x.dtype)
    om = jnp.einsum('bhln,bnhd->blhd', pr, mem_v.astype(x.dtype)).reshape(B, L, D_MEM)
    y_mem = om @ p['w_mem_out']
    gates = jax.nn.sigmoid(gate_raw).reshape(B, L, N_BRANCH, D_MODEL)
    merged = gates[:, :, 0] * y_conv + gates[:, :, 1] * y_hgrn + gates[:, :, 2] * y_mem
    x = x + merged @ p['w_out']
    x = x + hier_moe(rmsnorm(x, p['norm_ffn']), p['w_router_group'], p['b_router_group'],
                     p['w_router_expert'], p['b_router_expert'], p['w_exp_in'], p['w_exp_out'])
    return x, new_conv_state, S_new


def setup_inputs(seed: int = 0) -> dict:
    key = jax.random.key(seed)
    keys = iter(jax.random.split(key, 32))

    def nrm(shape, scale):
        return jax.random.normal(next(keys), shape, jnp.float32) * scale

    def gain(shape):
        return 1.0 + nrm(shape, 0.02)

    return {
        'x_prompt': nrm((BATCH, SEQ, D_MODEL), 1.0),
        'x_sample': nrm((DEC_BATCH, DEC_SEQ, D_MODEL), 1.0),
        'state_conv': nrm((DEPTH, DEC_BATCH, CONV_WIDTH - 1, D_CONV), 1.0),
        'state_hgrn': nrm((DEPTH, DEC_BATCH, HGRN_HEADS, HGRN_DK, HGRN_DV), 0.5),
        'cache_mem_k': nrm((DEPTH, DEC_BATCH, N_MEM, MEM_HEADS, MEM_HEAD_DIM), 1.0),
        'cache_mem_v': nrm((DEPTH, DEC_BATCH, N_MEM, MEM_HEADS, MEM_HEAD_DIM), 1.0),
        'mem_prompt': nrm((BATCH, N_MEM, D_MODEL), 1.0),
        'norm_mix': gain((DEPTH, D_MODEL)),
        'norm_mem': gain((DEPTH, D_MODEL)),
        'norm_ffn': gain((DEPTH, D_MODEL)),
        'norm_final': gain((D_MODEL,)),
        'w_in': nrm((DEPTH, D_MODEL, D_IN), D_MODEL ** -0.5),
        'w_dw': nrm((DEPTH, CONV_WIDTH, D_CONV), CONV_WIDTH ** -0.5),
        'b_dw': nrm((DEPTH, D_CONV), 0.02),
        'conv_ln_g': gain((DEPTH, D_CONV)),
        'conv_ln_b': nrm((DEPTH, D_CONV), 0.02),
        'w_conv_out': nrm((DEPTH, D_CONV, D_MODEL), D_CONV ** -0.5),
        'hgrn_lb': nrm((DEPTH, D_HGRN_K), 0.5),
        'hgrn_norm': gain((DEPTH, D_HGRN_V)),
        'w_hgrn_out': nrm((DEPTH, D_HGRN_V, D_MODEL), D_HGRN_V ** -0.5),
        'w_mem_kv': nrm((DEPTH, D_MODEL, 2 * D_MEM), D_MODEL ** -0.5),
        'w_mem_out': nrm((DEPTH, D_MEM, D_MODEL), D_MEM ** -0.5),
        'w_out': nrm((DEPTH, D_MODEL, D_MODEL), D_MODEL ** -0.5),
        'w_router_group': nrm((DEPTH, D_MODEL, N_GROUPS), D_MODEL ** -0.5),
        'b_router_group': nrm((DEPTH, N_GROUPS), 0.01),
        'w_router_expert': nrm((DEPTH, D_MODEL, N_EXPERTS), D_MODEL ** -0.5),
        'b_router_expert': nrm((DEPTH, N_EXPERTS), 0.01),
        'w_exp_in': nrm((DEPTH, N_EXPERTS, D_MODEL, 2 * D_EXPERT), D_MODEL ** -0.5),
        'w_exp_out': nrm((DEPTH, N_EXPERTS, D_EXPERT, D_MODEL), D_EXPERT ** -0.5),
    }


def reference(x_prompt, x_sample, state_conv, state_hgrn, cache_mem_k, cache_mem_v, mem_prompt,
              norm_mix, norm_mem, norm_ffn, norm_final, w_in, w_dw, b_dw, conv_ln_g, conv_ln_b, w_conv_out,
              hgrn_lb, hgrn_norm, w_hgrn_out, w_mem_kv, w_mem_out, w_out,
              w_router_group, b_router_group, w_router_expert, b_router_expert, w_exp_in, w_exp_out):
    lb_all = jnp.cumsum(jax.nn.softmax(hgrn_lb.astype(jnp.float32), axis=0), axis=0)
    lb_all = lb_all - lb_all[:1]
    B_p = x_prompt.shape[0]
    y_p, y_s = x_prompt, x_sample
    conv_p, hgrn_p, mk_p, mv_p, conv_s, hgrn_s = [], [], [], [], [], []
    for l in range(DEPTH):
        p = dict(norm_mix=norm_mix[l], w_in=w_in[l], w_dw=w_dw[l], b_dw=b_dw[l], conv_ln_g=conv_ln_g[l],
                 conv_ln_b=conv_ln_b[l], w_conv_out=w_conv_out[l], hgrn_norm=hgrn_norm[l], w_hgrn_out=w_hgrn_out[l],
                 w_mem_out=w_mem_out[l], w_out=w_out[l], norm_ffn=norm_ffn[l], w_router_group=w_router_group[l],
                 b_router_group=b_router_group[l], w_router_expert=w_router_expert[l],
                 b_router_expert=b_router_expert[l], w_exp_in=w_exp_in[l], w_exp_out=w_exp_out[l])
        mk, mv = jnp.split(rmsnorm(mem_prompt, norm_mem[l]) @ w_mem_kv[l], 2, axis=-1)
        mk = mk.reshape(B_p, N_MEM, MEM_HEADS, MEM_HEAD_DIM)
        mv = mv.reshape(B_p, N_MEM, MEM_HEADS, MEM_HEAD_DIM)
        conv0 = jnp.zeros((B_p, CONV_WIDTH - 1, D_CONV), x_prompt.dtype)
        S0 = jnp.zeros((B_p, HGRN_HEADS, HGRN_DK, HGRN_DV), jnp.float32)
        y_p, cs_p, hs_p = layer(y_p, conv0, S0, mk, mv, lb_all[l], p)
        y_s, cs_s, hs_s = layer(y_s, state_conv[l], state_hgrn[l], cache_mem_k[l], cache_mem_v[l], lb_all[l], p)
        conv_p.append(cs_p)
        hgrn_p.append(hs_p.astype(state_hgrn.dtype))
        mk_p.append(mk)
        mv_p.append(mv)
        conv_s.append(cs_s.astype(state_conv.dtype))
        hgrn_s.append(hs_s.astype(state_hgrn.dtype))
    y_prompt = rmsnorm(y_p, norm_final)
    y_sample = rmsnorm(y_s, norm_final)
    return (y_prompt, y_sample, jnp.stack(conv_p), jnp.stack(hgrn_p), jnp.stack(mk_p), jnp.stack(mv_p),
            jnp.stack(conv_s), jnp.stack(hgrn_s))
```

```python
import functools

import numpy as np
import jax
import jax.numpy as jnp
from jax import lax
from jax.experimental import pallas as pl
from jax.experimental.pallas import tpu as pltpu

F32 = jnp.float32
BF16 = jnp.bfloat16

EPS = 1e-6
CONV_WIDTH = 31
HGRN_HEADS = 16
HGRN_DK = 128
HGRN_DV = 128
MEM_HEADS = 4
N_GROUPS = 4
EXPERTS_PER_GROUP = 8
N_EXPERTS = N_GROUPS * EXPERTS_PER_GROUP
TOP_K = 2

LANES = 128
SUBLANES = 8
VMEM_LIMIT_BYTES = 56 * 1024 * 1024
MM_TN = 512
MOE_TM = 256
CONV_PAD = 32
HGRN_CHUNK = 128


def _cparams(*sem):
    return pltpu.CompilerParams(dimension_semantics=sem, vmem_limit_bytes=VMEM_LIMIT_BYTES)


def _pick_tile(n, cands):
    for c in cands:
        if n % c == 0:
            return c
    return n


def _sigmoid(x):
    return 1.0 / (1.0 + jnp.exp(-x))


def _dot(a, b):
    return jnp.dot(a, b, preferred_element_type=F32)


def _dot_nt(a, b):
    return lax.dot_general(a, b, (((1,), (1,)), ((), ())), preferred_element_type=F32)


def _dot_tn(a, b):
    return lax.dot_general(a, b, (((0,), (0,)), ((), ())), preferred_element_type=F32)


def _rms_body(x, g):
    ms = jnp.mean(x * x, axis=-1, keepdims=True)
    return x * lax.rsqrt(ms + EPS) * g


def _rms_kernel(x_ref, g_ref, o_ref):
    o_ref[...] = _rms_body(x_ref[...], g_ref[...]).astype(o_ref.dtype)


def rmsnorm(x, g, out_dtype):
    T, D = x.shape
    tm = _pick_tile(T, (320, 256, 128, 64, 32, 16, 8))
    return pl.pallas_call(
        _rms_kernel,
        out_shape=jax.ShapeDtypeStruct((T, D), out_dtype),
        grid=(T // tm,),
        in_specs=[pl.BlockSpec((tm, D), lambda i: (i, 0)), pl.BlockSpec((1, D), lambda i: (0, 0))],
        out_specs=pl.BlockSpec((tm, D), lambda i: (i, 0)),
        compiler_params=_cparams("parallel"),
        name="rmsnorm",
    )(x, g.reshape(1, D))


def _rms_router_kernel(x_ref, g_ref, wr_ref, h_ref, lg_ref):
    h = _rms_body(x_ref[...], g_ref[...])
    h_ref[...] = h.astype(h_ref.dtype)
    lg_ref[...] = jnp.dot(h, wr_ref[...], precision=lax.Precision.HIGHEST, preferred_element_type=F32)


def rmsnorm_router(x, g, w_router):
    T, D = x.shape
    R = w_router.shape[1]
    tm = _pick_tile(T, (320, 256, 128, 64, 32, 16, 8))
    return pl.pallas_call(
        _rms_router_kernel,
        out_shape=(jax.ShapeDtypeStruct((T, D), BF16), jax.ShapeDtypeStruct((T, R), F32)),
        grid=(T // tm,),
        in_specs=[pl.BlockSpec((tm, D), lambda i: (i, 0)), pl.BlockSpec((1, D), lambda i: (0, 0)),
                  pl.BlockSpec((D, R), lambda i: (0, 0))],
        out_specs=(pl.BlockSpec((tm, D), lambda i: (i, 0)), pl.BlockSpec((tm, R), lambda i: (i, 0))),
        compiler_params=_cparams("parallel"),
        name="rmsnorm_router",
    )(x, g.reshape(1, D), w_router)


def _add3_kernel(x_ref, a_ref, b_ref, o_ref):
    o_ref[...] = x_ref[...] + (a_ref[...] + b_ref[...])


def add3(x, a, b):
    T, D = x.shape
    tm = _pick_tile(T, (320, 256, 128, 64, 32, 16, 8))
    spec = pl.BlockSpec((tm, D), lambda i: (i, 0))
    return pl.pallas_call(
        _add3_kernel, out_shape=jax.ShapeDtypeStruct((T, D), F32), grid=(T // tm,),
        in_specs=[spec, spec, spec], out_specs=spec, compiler_params=_cparams("parallel"), name="moe_combine",
    )(x, a, b)


def _mm_kernel(*refs, n_terms, gated, residual):
    a_refs = refs[:n_terms]
    w_refs = refs[n_terms:2 * n_terms]
    pos = 2 * n_terms
    g_refs = refs[pos:pos + n_terms] if gated else ()
    pos += n_terms if gated else 0
    r_ref = refs[pos] if residual else None
    pos += 1 if residual else 0
    o_ref = refs[pos]
    wb_refs = refs[pos + 1:pos + 1 + n_terms]

    @pl.when(pl.program_id(1) == 0)
    def _():
        for w_ref, wb_ref in zip(w_refs, wb_refs):
            wb_ref[...] = w_ref[...].astype(BF16)

    acc = None
    for i in range(n_terms):
        y = _dot(a_refs[i][...], wb_refs[i][...])
        if gated:
            y = y * _sigmoid(g_refs[i][...])
        acc = y if acc is None else acc + y
    if residual:
        acc = acc + r_ref[...]
    o_ref[...] = acc.astype(o_ref.dtype)


def dense(a_list, w_list, out_dtype, gate_src=None, gate_cols=None, residual=None, name="dense"):
    n_terms = len(a_list)
    T = a_list[0].shape[0]
    N = w_list[0].shape[1]
    tn = MM_TN
    tm = _pick_tile(T, (640, 512, 256, 128, 64, 32, 16, 8))
    in_specs, args = [], []
    for a in a_list:
        in_specs.append(pl.BlockSpec((tm, a.shape[1]), lambda n, m: (m, 0)))
        args.append(a)
    for w in w_list:
        in_specs.append(pl.BlockSpec((w.shape[0], tn), lambda n, m: (0, n)))
        args.append(w)
    if gate_src is not None:
        for c0 in gate_cols:
            in_specs.append(pl.BlockSpec((tm, tn), functools.partial(lambda n, m, cb: (m, cb + n), cb=c0 // tn)))
            args.append(gate_src)
    if residual is not None:
        in_specs.append(pl.BlockSpec((tm, tn), lambda n, m: (m, n)))
        args.append(residual)
    kern = functools.partial(_mm_kernel, n_terms=n_terms, gated=gate_src is not None,
                             residual=residual is not None)
    return pl.pallas_call(
        kern,
        out_shape=jax.ShapeDtypeStruct((T, N), out_dtype),
        grid=(N // tn, T // tm),
        in_specs=in_specs,
        out_specs=pl.BlockSpec((tm, tn), lambda n, m: (m, n)),
        scratch_shapes=[pltpu.VMEM((w.shape[0], tn), BF16) for w in w_list],
        compiler_params=_cparams("arbitrary", "arbitrary"),
        name=name,
    )(*args)


def _conv_kernel(a_ref, b_ref, st_ref, w_ref, bdw_ref, lng_ref, lnb_ref, c_ref, ns_ref, ubuf, cbuf, *, tl, rc):
    W1 = CONV_WIDTH - 1
    lo = CONV_PAD - W1

    @pl.when(pl.program_id(1) == 0)
    def _():
        ubuf[lo:CONV_PAD, :] = st_ref[...]

    ubuf[CONV_PAD:CONV_PAD + tl, :] = a_ref[...] * _sigmoid(b_ref[...])
    dc = ubuf.shape[1]
    lc = 256

    def chunk(i, carry):
        r0 = pl.multiple_of(i * rc, SUBLANES)
        for c0 in range(0, dc, lc):
            win = ubuf[pl.ds(r0, rc + CONV_PAD), c0:c0 + lc]
            acc = jnp.zeros((rc, lc), F32)
            for j in range(CONV_WIDTH):
                acc = acc + win[lo + j:lo + j + rc, :] * w_ref[j:j + 1, c0:c0 + lc]
            cbuf[pl.ds(r0, rc), c0:c0 + lc] = acc
        y = cbuf[pl.ds(r0, rc), :] + bdw_ref[...]
        mu = jnp.mean(y, axis=-1, keepdims=True)
        d = y - mu
        var = jnp.mean(d * d, axis=-1, keepdims=True)
        z = d * lax.rsqrt(var + EPS) * lng_ref[...] + lnb_ref[...]
        c_ref[pl.ds(r0, rc), :] = (z * _sigmoid(z)).astype(c_ref.dtype)
        return carry

    lax.fori_loop(0, tl // rc, chunk, 0)
    tail = ubuf[tl + lo:tl + CONV_PAD, :]
    ns_ref[...] = tail
    ubuf[lo:CONV_PAD, :] = tail


def conv_branch(proj, row0, B, L, state, w_dw, b_dw, ln_g, ln_b):
    dc = w_dw.shape[1]
    tl = _pick_tile(L, (512, 256, 128, 64, 32, 16))
    rc = min(tl, 32)
    nl = L // tl
    rb0 = row0 // tl
    kern = functools.partial(_conv_kernel, tl=tl, rc=rc)
    vec = pl.BlockSpec((1, dc), lambda b, l: (0, 0))
    return pl.pallas_call(
        kern,
        out_shape=(jax.ShapeDtypeStruct((B * L, dc), BF16),
                   jax.ShapeDtypeStruct((B, CONV_WIDTH - 1, dc), F32)),
        grid=(B, nl),
        in_specs=[pl.BlockSpec((tl, dc), lambda b, l: (rb0 + b * nl + l, 0)),
                  pl.BlockSpec((tl, dc), lambda b, l: (rb0 + b * nl + l, 1)),
                  pl.BlockSpec((None, CONV_WIDTH - 1, dc), lambda b, l: (b, 0, 0)),
                  pl.BlockSpec((CONV_WIDTH, dc), lambda b, l: (0, 0)), vec, vec, vec],
        out_specs=(pl.BlockSpec((tl, dc), lambda b, l: (b * nl + l, 0)),
                   pl.BlockSpec((None, CONV_WIDTH - 1, dc), lambda b, l: (b, 0, 0))),
        scratch_shapes=[pltpu.VMEM((tl + CONV_PAD, dc), F32), pltpu.VMEM((tl, dc), F32)],
        compiler_params=_cparams("arbitrary", "arbitrary"),
        name="conv_branch",
    )(proj, proj, state, w_dw, b_dw.reshape(1, dc), ln_g.reshape(1, dc), ln_b.reshape(1, dc))


def _pair_levels(C):
    t = np.arange(C)[:, None]
    s = np.arange(C)[None, :]
    x = np.bitwise_xor(t, s)
    lvl = np.where(x > 0, np.floor(np.log2(np.maximum(x, 1))).astype(np.int32) + 1, 0)
    return np.where(s <= t, lvl, -1).astype(np.int32)


def _group_boundary(b_ref, hs, m, C):
    if 2 * m >= SUBLANES:
        pieces = [jnp.broadcast_to(b_ref[pl.ds(g * 2 * m + m - 1, 1), hs], (2 * m, LANES))
                  for g in range(C // (2 * m))]
        return pieces[0] if len(pieces) == 1 else jnp.concatenate(pieces, axis=0)
    sub = lax.broadcasted_iota(jnp.int32, (SUBLANES, LANES), 0)
    pieces = []
    for v in range(C // SUBLANES):
        piece = jnp.broadcast_to(b_ref[pl.ds(v * SUBLANES + m - 1, 1), hs], (SUBLANES, LANES))
        for j in range(1, SUBLANES // (2 * m)):
            nxt = jnp.broadcast_to(b_ref[pl.ds(v * SUBLANES + j * 2 * m + m - 1, 1), hs], (SUBLANES, LANES))
            piece = jnp.where(sub >= j * 2 * m, nxt, piece)
        pieces.append(piece)
    return jnp.concatenate(pieces, axis=0)


def _hgrn_kernel(f_ref, i_ref, q_ref, g_ref, s0_ref, loglb_ref, log1m_ref, nw_ref, tri_ref, lvl_ref,
                 o_ref, sout_ref, st_ref, lf_ref, b_ref, *, C):
    cidx = pl.program_id(1)

    @pl.when(cidx == 0)
    def _():
        def init(h, carry):
            st_ref[h] = s0_ref[h].T
            return carry
        lax.fori_loop(0, HGRN_HEADS, init, 0)

    x = f_ref[...]
    log_sig = jnp.minimum(x, 0.0) - jnp.log(1.0 + jnp.exp(-jnp.abs(x)))
    la = loglb_ref[...]
    lc = log1m_ref[...] + log_sig
    lf = jnp.maximum(la, lc) + jnp.log(1.0 + jnp.exp(-jnp.abs(la - lc)))
    lf_ref[...] = lf
    hi = lf.astype(BF16)
    lo = (lf - hi.astype(F32)).astype(BF16)
    b_ref[...] = _dot(tri_ref[...], hi) + _dot(tri_ref[...], lo)

    row = lax.broadcasted_iota(jnp.int32, (C, LANES), 0)

    def head(h, carry):
        hs = pl.ds(pl.multiple_of(h * LANES, LANES), LANES)
        lf_h = lf_ref[:, hs]
        b = b_ref[:, hs]
        q = q_ref[:, hs]
        v = i_ref[:, hs].astype(BF16)
        k = 1.0 - jnp.exp(lf_h)
        lvl = lvl_ref[...]
        p = jnp.where(lvl == 0, _dot_nt(q.astype(BF16), k.astype(BF16)), 0.0)
        m, level = 1, 1
        while m < C:
            upper = (row & m) != 0
            if m == 1:
                xpo = jnp.where(upper, lf_h, 0.0)
            else:
                bnd = _group_boundary(b_ref, hs, m, C)
                xpo = jnp.where(upper, b - bnd, bnd - b)
            ex = jnp.exp(xpo)
            pm = _dot_nt((q * ex).astype(BF16), (k * ex).astype(BF16))
            p = jnp.where(lvl == level, pm, p)
            m *= 2
            level += 1
        s_t = st_ref[h]
        o = _dot(p.astype(BF16), v) + _dot_nt((q * jnp.exp(b)).astype(BF16), s_t.astype(BF16))
        b_last = b_ref[pl.ds(C - 1, 1), hs]
        k_dec = (k * jnp.exp(b_last - b)).astype(BF16)
        st_ref[h] = s_t * jnp.exp(b_last) + _dot_tn(v, k_dec)
        ms = jnp.mean(o * o, axis=-1, keepdims=True)
        gv = g_ref[:, hs]
        o_ref[:, hs] = (o * lax.rsqrt(ms + EPS) * nw_ref[:, hs] * (gv * _sigmoid(gv))).astype(o_ref.dtype)
        return carry

    lax.fori_loop(0, HGRN_HEADS, head, 0)

    @pl.when(cidx == pl.num_programs(1) - 1)
    def _():
        def fin(h, carry):
            sout_ref[h] = st_ref[h].T
            return carry
        lax.fori_loop(0, HGRN_HEADS, fin, 0)


def hgrn_branch(proj, row0, B, L, s0, log_lb, log1m_lb, norm_w, col0):
    dk = HGRN_HEADS * HGRN_DK
    C = _pick_tile(L, (HGRN_CHUNK, 64, 32, 16))
    nc = L // C
    rb0 = row0 // C
    cb0 = col0 // dk
    tri = jnp.asarray(np.tril(np.ones((C, C), np.float32)), BF16)
    lvl = jnp.asarray(_pair_levels(C))
    kern = functools.partial(_hgrn_kernel, C=C)

    def pspec(j):
        return pl.BlockSpec((C, dk), functools.partial(lambda b, c, j: (rb0 + b * nc + c, cb0 + j), j=j))

    vec = pl.BlockSpec((1, dk), lambda b, c: (0, 0))
    sspec = pl.BlockSpec((None, HGRN_HEADS, HGRN_DK, HGRN_DV), lambda b, c: (b, 0, 0, 0))
    return pl.pallas_call(
        kern,
        out_shape=(jax.ShapeDtypeStruct((B * L, dk), BF16),
                   jax.ShapeDtypeStruct((B, HGRN_HEADS, HGRN_DK, HGRN_DV), F32)),
        grid=(B, nc),
        in_specs=[pspec(0), pspec(1), pspec(2), pspec(3), sspec, vec, vec, vec,
                  pl.BlockSpec((C, C), lambda b, c: (0, 0)), pl.BlockSpec((C, C), lambda b, c: (0, 0))],
        out_specs=(pl.BlockSpec((C, dk), lambda b, c: (b * nc + c, 0)), sspec),
        scratch_shapes=[pltpu.VMEM((HGRN_HEADS, HGRN_DV, HGRN_DK), F32), pltpu.VMEM((C, dk), F32),
                        pltpu.VMEM((C, dk), F32)],
        compiler_params=_cparams("arbitrary", "arbitrary"),
        name="hgrn_branch",
    )(proj, proj, proj, proj, s0, log_lb.reshape(1, dk), log1m_lb.reshape(1, dk), norm_w.reshape(1, dk), tri, lvl)


def _attn_kernel(q_ref, k_ref, v_ref, o_ref, *, scale):
    hd = q_ref.shape[1] // MEM_HEADS
    for h in range(MEM_HEADS):
        cs = slice(h * hd, (h + 1) * hd)
        s = _dot_nt(q_ref[:, cs].astype(BF16), k_ref[:, cs].astype(BF16)) * scale
        s = s - jnp.max(s, axis=-1, keepdims=True)
        e = jnp.exp(s)
        pr = e / jnp.sum(e, axis=-1, keepdims=True)
        o_ref[:, cs] = _dot(pr.astype(BF16), v_ref[:, cs].astype(BF16)).astype(o_ref.dtype)


def attn_branch(proj, row0, B, L, mem_k, mem_v, col0):
    n_mem, dm = mem_k.shape[1], mem_k.shape[2]
    tl = _pick_tile(L, (512, 256, 128, 64, 32, 16))
    nl = L // tl
    rb0 = row0 // tl
    cb0 = col0 // dm
    kern = functools.partial(_attn_kernel, scale=float((dm // MEM_HEADS) ** -0.5))
    mspec = pl.BlockSpec((None, n_mem, dm), lambda b, l: (b, 0, 0))
    return pl.pallas_call(
        kern,
        out_shape=jax.ShapeDtypeStruct((B * L, dm), BF16),
        grid=(B, nl),
        in_specs=[pl.BlockSpec((tl, dm), lambda b, l: (rb0 + b * nl + l, cb0)), mspec, mspec],
        out_specs=pl.BlockSpec((tl, dm), lambda b, l: (b * nl + l, 0)),
        compiler_params=_cparams("parallel", "parallel"),
        name="attn_branch",
    )(proj, mem_k, mem_v)


def _moe_in_kernel(be_ref, nv_ref, x_ref, wa_ref, wu_ref, o_ref, wab_ref, wub_ref):
    b = pl.program_id(1)
    fresh = jnp.logical_or(b == 0, be_ref[b] != be_ref[jnp.maximum(b - 1, 0)])

    @pl.when(fresh)
    def _():
        wab_ref[...] = wa_ref[...].astype(BF16)
        wub_ref[...] = wu_ref[...].astype(BF16)

    @pl.when(b < nv_ref[0])
    def _():
        x = x_ref[...]
        a = _dot(x, wab_ref[...])
        u = _dot(x, wub_ref[...])
        o_ref[...] = (a * _sigmoid(a) * u).astype(o_ref.dtype)

    @pl.when(b >= nv_ref[0])
    def _():
        o_ref[...] = jnp.zeros_like(o_ref)


def _moe_out_kernel(be_ref, nv_ref, h_ref, w_ref, g_ref, o_ref, wb_ref):
    b = pl.program_id(1)
    fresh = jnp.logical_or(b == 0, be_ref[b] != be_ref[jnp.maximum(b - 1, 0)])

    @pl.when(fresh)
    def _():
        wb_ref[...] = w_ref[...].astype(BF16)

    @pl.when(b < nv_ref[0])
    def _():
        o_ref[...] = _dot(h_ref[...], wb_ref[...]) * g_ref[...]

    @pl.when(b >= nv_ref[0])
    def _():
        o_ref[...] = jnp.zeros_like(o_ref)


def moe_experts(xs, blk_e, n_valid, buf_gate, w_ein, w_eout):
    P, D = xs.shape
    de = w_eout.shape[1]
    tm = MOE_TM
    nb = P // tm
    tf = 512
    nj = de // tf
    hmid = pl.pallas_call(
        _moe_in_kernel,
        out_shape=jax.ShapeDtypeStruct((P, de), BF16),
        grid_spec=pltpu.PrefetchScalarGridSpec(
            num_scalar_prefetch=2, grid=(nj, nb),
            in_specs=[pl.BlockSpec((tm, D), lambda j, b, be, nv: (jnp.minimum(b, nv[0] - 1), 0)),
                      pl.BlockSpec((None, D, tf), lambda j, b, be, nv: (be[b], 0, j)),
                      pl.BlockSpec((None, D, tf), lambda j, b, be, nv: (be[b], 0, nj + j))],
            out_specs=pl.BlockSpec((tm, tf), lambda j, b, be, nv: (b, j)),
            scratch_shapes=[pltpu.VMEM((D, tf), BF16), pltpu.VMEM((D, tf), BF16)]),
        compiler_params=_cparams("arbitrary", "arbitrary"),
        name="moe_in",
    )(blk_e, n_valid, xs, w_ein, w_ein)
    tn = 2048
    return pl.pallas_call(
        _moe_out_kernel,
        out_shape=jax.ShapeDtypeStruct((P, D), F32),
        grid_spec=pltpu.PrefetchScalarGridSpec(
            num_scalar_prefetch=2, grid=(D // tn, nb),
            in_specs=[pl.BlockSpec((tm, de), lambda n, b, be, nv: (jnp.minimum(b, nv[0] - 1), 0)),
                      pl.BlockSpec((None, de, tn), lambda n, b, be, nv: (be[b], 0, n)),
                      pl.BlockSpec((tm, 1), lambda n, b, be, nv: (b, 0))],
            out_specs=pl.BlockSpec((tm, tn), lambda n, b, be, nv: (b, n)),
            scratch_shapes=[pltpu.VMEM((de, tn), BF16)]),
        compiler_params=_cparams("arbitrary", "arbitrary"),
        name="moe_out",
    )(blk_e, n_valid, hmid, w_eout, buf_gate.reshape(P, 1))


def _route(logits, b_rg, b_re):
    T = logits.shape[0]
    g_logits = logits[:, :N_GROUPS] + b_rg
    g_prob = jax.nn.softmax(g_logits, axis=-1)
    grp = jnp.argmax(g_logits, axis=-1).astype(jnp.int32)
    p_grp = jnp.take_along_axis(g_prob, grp[:, None], axis=-1)
    e_logits = (logits[:, N_GROUPS:N_GROUPS + N_EXPERTS] + b_re).reshape(T, N_GROUPS, EXPERTS_PER_GROUP)
    e_sel = jnp.take_along_axis(e_logits, grp[:, None, None], axis=1)[:, 0]
    top_val, top_idx = lax.top_k(e_sel, TOP_K)
    gate = p_grp * jax.nn.softmax(top_val, axis=-1)
    eid = grp[:, None] * EXPERTS_PER_GROUP + top_idx
    return eid, gate


def hier_moe(x, norm_w, w_rg, b_rg, w_re, b_re, w_ein, w_eout):
    T, D = x.shape
    tm = MOE_TM
    w_router = jnp.zeros((D, LANES), F32).at[:, :N_GROUPS].set(w_rg).at[:, N_GROUPS:N_GROUPS + N_EXPERTS].set(w_re)
    h, logits = rmsnorm_router(x, norm_w, w_router)
    eid, gate = _route(logits, b_rg, b_re)
    A = T * TOP_K
    flat_e = eid.reshape(-1)
    onehot = (flat_e[:, None] == jnp.arange(N_EXPERTS, dtype=jnp.int32)[None, :]).astype(jnp.int32)
    csum = jnp.cumsum(onehot, axis=0)
    rank = jnp.take_along_axis(csum, flat_e[:, None], axis=1)[:, 0] - 1
    counts = csum[-1]
    padded = (counts + tm - 1) // tm * tm
    pend = jnp.cumsum(padded)
    pstart = pend - padded
    dest = pstart[flat_e] + rank
    nb = -(-(A + N_EXPERTS * (tm - 1)) // tm)
    P = nb * tm
    flat_tok = jnp.repeat(jnp.arange(T, dtype=jnp.int32), TOP_K)
    buf_tok = jnp.zeros((P,), jnp.int32).at[dest].set(flat_tok)
    buf_gate = jnp.zeros((P,), F32).at[dest].set(gate.reshape(-1))
    n_valid = (pend[-1] // tm).astype(jnp.int32)
    blk_raw = jnp.minimum(jnp.sum(jnp.arange(nb, dtype=jnp.int32)[:, None] * tm >= pend[None, :], axis=1),
                          N_EXPERTS - 1).astype(jnp.int32)
    blk_e = jnp.where(jnp.arange(nb) < n_valid, blk_raw, blk_raw[jnp.maximum(n_valid - 1, 0)])
    xs = jnp.take(h, buf_tok, axis=0)
    y = moe_experts(xs, blk_e, n_valid.reshape(1), buf_gate, w_ein, w_eout)
    dest2 = dest.reshape(T, TOP_K)
    return add3(x, jnp.take(y, dest2[:, 0], axis=0), jnp.take(y, dest2[:, 1], axis=0))


def _layer(x, groups, log_lb, log1m_lb, p):
    dc = p['w_dw'].shape[1]
    dk = HGRN_HEADS * HGRN_DK
    dm = p['w_mem_out'].shape[0]
    D = x.shape[1]
    h = rmsnorm(x, p['norm_mix'], BF16)
    proj = dense([h], [p['w_in']], F32, name="in_proj")
    col_hgrn = 2 * dc
    col_mem = col_hgrn + 4 * dk
    col_gate = col_mem + dm
    cs, os_, oms, conv_states, hgrn_states = [], [], [], [], []
    for row0, B, L, conv_state, s0, mem_k, mem_v in groups:
        c, ns = conv_branch(proj, row0, B, L, conv_state, p['w_dw'], p['b_dw'], p['conv_ln_g'], p['conv_ln_b'])
        o, s_new = hgrn_branch(proj, row0, B, L, s0, log_lb, log1m_lb, p['hgrn_norm'], col_hgrn)
        om = attn_branch(proj, row0, B, L, mem_k.reshape(B, -1, dm), mem_v.reshape(B, -1, dm), col_mem)
        cs.append(c); os_.append(o); oms.append(om); conv_states.append(ns); hgrn_states.append(s_new)
    cat = lambda xs: xs[0] if len(xs) == 1 else jnp.concatenate(xs, axis=0)
    merged = dense([cat(cs), cat(os_), cat(oms)], [p['w_conv_out'], p['w_hgrn_out'], p['w_mem_out']], BF16,
                   gate_src=proj, gate_cols=[col_gate, col_gate + D, col_gate + 2 * D], name="branch_merge")
    x = dense([merged], [p['w_out']], F32, residual=x, name="out_proj")
    x = hier_moe(x, p['norm_ffn'], p['w_router_group'], p['b_router_group'], p['w_router_expert'],
                 p['b_router_expert'], p['w_exp_in'], p['w_exp_out'])
    return x, conv_states, hgrn_states


def kernel(x_prompt, x_sample, state_conv, state_hgrn, cache_mem_k, cache_mem_v, mem_prompt, norm_mix, norm_mem, norm_ffn, norm_final, w_in, w_dw, b_dw, conv_ln_g, conv_ln_b, w_conv_out, hgrn_lb, hgrn_norm, w_hgrn_out, w_mem_kv, w_mem_out, w_out, w_router_group, b_router_group, w_router_expert, b_router_expert, w_exp_in, w_exp_out):
    depth = w_in.shape[0]
    Bp, Lp, D = x_prompt.shape
    Bs, Ls, _ = x_sample.shape
    n_mem = mem_prompt.shape[1]
    dc = w_dw.shape[2]
    dm = w_mem_out.shape[1]
    lb_all = jnp.cumsum(jax.nn.softmax(hgrn_lb.astype(F32), axis=0), axis=0)
    lb_all = lb_all - lb_all[:1]
    log_lb = jnp.log(lb_all)
    log1m_lb = jnp.log1p(-lb_all)
    x = jnp.concatenate([x_prompt.reshape(Bp * Lp, D), x_sample.reshape(Bs * Ls, D)], axis=0)
    conv0 = jnp.zeros((Bp, CONV_WIDTH - 1, dc), F32)
    s0 = jnp.zeros((Bp, HGRN_HEADS, HGRN_DK, HGRN_DV), F32)
    mem_flat = mem_prompt.reshape(Bp * n_mem, D)
    conv_p, hgrn_p, mk_p, mv_p, conv_s, hgrn_s = [], [], [], [], [], []
    for l in range(depth):
        p = dict(norm_mix=norm_mix[l], w_in=w_in[l], w_dw=w_dw[l], b_dw=b_dw[l], conv_ln_g=conv_ln_g[l],
                 conv_ln_b=conv_ln_b[l], w_conv_out=w_conv_out[l], hgrn_norm=hgrn_norm[l], w_hgrn_out=w_hgrn_out[l],
                 w_mem_out=w_mem_out[l], w_out=w_out[l], norm_ffn=norm_ffn[l], w_router_group=w_router_group[l],
                 b_router_group=b_router_group[l], w_router_expert=w_router_expert[l],
                 b_router_expert=b_router_expert[l], w_exp_in=w_exp_in[l], w_exp_out=w_exp_out[l])
        kv = dense([rmsnorm(mem_flat, norm_mem[l], BF16)], [w_mem_kv[l]], F32, name="mem_kv")
        mk = kv[:, :dm].reshape(Bp, n_mem, MEM_HEADS, dm // MEM_HEADS)
        mv = kv[:, dm:].reshape(Bp, n_mem, MEM_HEADS, dm // MEM_HEADS)
        groups = [(0, Bp, Lp, conv0, s0, mk, mv),
                  (Bp * Lp, Bs, Ls, state_conv[l], state_hgrn[l], cache_mem_k[l], cache_mem_v[l])]
        x, cstates, hstates = _layer(x, groups, log_lb[l], log1m_lb[l], p)
        conv_p.append(cstates[0]); hgrn_p.append(hstates[0]); mk_p.append(mk); mv_p.append(mv)
        conv_s.append(cstates[1]); hgrn_s.append(hstates[1])
    y = rmsnorm(x, norm_final, F32)
    y_prompt = y[:Bp * Lp].reshape(Bp, Lp, D)
    y_sample = y[Bp * Lp:].reshape(Bs, Ls, D)
    return (y_prompt, y_sample, jnp.stack(conv_p), jnp.stack(hgrn_p), jnp.stack(mk_p), jnp.stack(mv_p),
            jnp.stack(conv_s), jnp.stack(hgrn_s))
```

```python
import functools

import numpy as np
import jax
import jax.numpy as jnp
from jax import lax
from jax.experimental import pallas as pl
from jax.experimental.pallas import tpu as pltpu

F32 = jnp.float32
BF16 = jnp.bfloat16

EPS = 1e-6
CONV_WIDTH = 31
HGRN_HEADS = 16
HGRN_DK = 128
HGRN_DV = 128
MEM_HEADS = 4
N_GROUPS = 4
EXPERTS_PER_GROUP = 8
N_EXPERTS = N_GROUPS * EXPERTS_PER_GROUP
TOP_K = 2

LANES = 128
SUBLANES = 8
VMEM_LIMIT_BYTES = 56 * 1024 * 1024
MM_TN = 512
MOE_TM = 256
CONV_PAD = 32
HGRN_CHUNK = 128


def _cparams(*sem):
    return pltpu.CompilerParams(dimension_semantics=sem, vmem_limit_bytes=VMEM_LIMIT_BYTES)


def _pick_tile(n, cands):
    for c in cands:
        if n % c == 0:
            return c
    return n


def _sigmoid(x):
    return 1.0 / (1.0 + jnp.exp(-x))


def _dot(a, b):
    return jnp.dot(a, b, preferred_element_type=F32)


def _dot_nt(a, b):
    return lax.dot_general(a, b, (((1,), (1,)), ((), ())), preferred_element_type=F32)


def _dot_tn(a, b):
    return lax.dot_general(a, b, (((0,), (0,)), ((), ())), preferred_element_type=F32)


def _rms_body(x, g):
    ms = jnp.mean(x * x, axis=-1, keepdims=True)
    return x * lax.rsqrt(ms + EPS) * g


def _rms_kernel(x_ref, g_ref, o_ref):
    o_ref[...] = _rms_body(x_ref[...], g_ref[...]).astype(o_ref.dtype)


def rmsnorm(x, g, out_dtype):
    T, D = x.shape
    tm = _pick_tile(T, (320, 256, 128, 64, 32, 16, 8))
    return pl.pallas_call(
        _rms_kernel,
        out_shape=jax.ShapeDtypeStruct((T, D), out_dtype),
        grid=(T // tm,),
        in_specs=[pl.BlockSpec((tm, D), lambda i: (i, 0)), pl.BlockSpec((1, D), lambda i: (0, 0))],
        out_specs=pl.BlockSpec((tm, D), lambda i: (i, 0)),
        compiler_params=_cparams("parallel"),
        name="rmsnorm",
    )(x, g.reshape(1, D))


def _pack_pairs(h):
    half = h.shape[1] // 2
    bits = lax.bitcast_convert_type(h.astype(BF16).astype(F32), jnp.uint32)
    return bits[:, :half] | (bits[:, half:] >> 16)


def _unpack_pairs(w):
    hi = lax.bitcast_convert_type(w & jnp.uint32(0xFFFF0000), F32).astype(BF16)
    lo = lax.bitcast_convert_type(w << 16, F32).astype(BF16)
    return jnp.concatenate([hi, lo], axis=1)


def _rms_router_kernel(x_ref, g_ref, wr_ref, h_ref, lg_ref):
    h = _rms_body(x_ref[...], g_ref[...])
    h_ref[...] = _pack_pairs(h)
    lg_ref[...] = jnp.dot(h, wr_ref[...], precision=lax.Precision.HIGHEST, preferred_element_type=F32)


def rmsnorm_router(x, g, w_router):
    T, D = x.shape
    R = w_router.shape[1]
    tm = _pick_tile(T, (320, 256, 128, 64, 32, 16, 8))
    return pl.pallas_call(
        _rms_router_kernel,
        out_shape=(jax.ShapeDtypeStruct((T, D // 2), jnp.uint32), jax.ShapeDtypeStruct((T, R), F32)),
        grid=(T // tm,),
        in_specs=[pl.BlockSpec((tm, D), lambda i: (i, 0)), pl.BlockSpec((1, D), lambda i: (0, 0)),
                  pl.BlockSpec((D, R), lambda i: (0, 0))],
        out_specs=(pl.BlockSpec((tm, D // 2), lambda i: (i, 0)), pl.BlockSpec((tm, R), lambda i: (i, 0))),
        compiler_params=_cparams("parallel"),
        name="rmsnorm_router",
    )(x, g.reshape(1, D), w_router)


def _add3_kernel(x_ref, a_ref, b_ref, o_ref):
    o_ref[...] = x_ref[...] + (a_ref[...] + b_ref[...])


def add3(x, a, b):
    T, D = x.shape
    tm = _pick_tile(T, (320, 256, 128, 64, 32, 16, 8))
    spec = pl.BlockSpec((tm, D), lambda i: (i, 0))
    return pl.pallas_call(
        _add3_kernel, out_shape=jax.ShapeDtypeStruct((T, D), F32), grid=(T // tm,),
        in_specs=[spec, spec, spec], out_specs=spec, compiler_params=_cparams("parallel"), name="moe_combine",
    )(x, a, b)


def _mm_kernel(*refs, n_terms, gated, residual):
    a_refs = refs[:n_terms]
    w_refs = refs[n_terms:2 * n_terms]
    pos = 2 * n_terms
    g_refs = refs[pos:pos + n_terms] if gated else ()
    pos += n_terms if gated else 0
    r_ref = refs[pos] if residual else None
    pos += 1 if residual else 0
    o_ref = refs[pos]
    wb_refs = refs[pos + 1:pos + 1 + n_terms]

    @pl.when(pl.program_id(1) == 0)
    def _():
        for w_ref, wb_ref in zip(w_refs, wb_refs):
            wb_ref[...] = w_ref[...].astype(BF16)

    acc = None
    for i in range(n_terms):
        y = _dot(a_refs[i][...], wb_refs[i][...])
        if gated:
            y = y * _sigmoid(g_refs[i][...])
        acc = y if acc is None else acc + y
    if residual:
        acc = acc + r_ref[...]
    o_ref[...] = acc.astype(o_ref.dtype)


def dense(a_list, w_list, layer, out_dtype, gate_src=None, gate_cols=None, residual=None, name="dense",
          tn=MM_TN, tm_max=640):
    n_terms = len(a_list)
    T = a_list[0].shape[0]
    N = w_list[0].shape[2]
    tm = _pick_tile(T, tuple(c for c in (640, 512, 320, 256, 128, 64, 32, 16, 8) if c <= tm_max))
    in_specs, args = [], []
    for a in a_list:
        in_specs.append(pl.BlockSpec((tm, a.shape[1]), lambda n, m: (m, 0)))
        args.append(a)
    for w in w_list:
        in_specs.append(pl.BlockSpec((None, w.shape[1], tn), lambda n, m: (layer, 0, n)))
        args.append(w)
    if gate_src is not None:
        for c0 in gate_cols:
            in_specs.append(pl.BlockSpec((tm, tn), functools.partial(lambda n, m, cb: (m, cb + n), cb=c0 // tn)))
            args.append(gate_src)
    if residual is not None:
        in_specs.append(pl.BlockSpec((tm, tn), lambda n, m: (m, n)))
        args.append(residual)
    kern = functools.partial(_mm_kernel, n_terms=n_terms, gated=gate_src is not None,
                             residual=residual is not None)
    return pl.pallas_call(
        kern,
        out_shape=jax.ShapeDtypeStruct((T, N), out_dtype),
        grid=(N // tn, T // tm),
        in_specs=in_specs,
        out_specs=pl.BlockSpec((tm, tn), lambda n, m: (m, n)),
        scratch_shapes=[pltpu.VMEM((w.shape[1], tn), BF16) for w in w_list],
        compiler_params=_cparams("arbitrary", "arbitrary"),
        name=name,
    )(*args)


def _conv_kernel(a_ref, b_ref, st_ref, w_ref, bdw_ref, lng_ref, lnb_ref, c_ref, ns_ref, ubuf, cbuf, *, tl, rc):
    W1 = CONV_WIDTH - 1
    lo = CONV_PAD - W1

    @pl.when(pl.program_id(1) == 0)
    def _():
        ubuf[lo:CONV_PAD, :] = st_ref[...]

    ubuf[CONV_PAD:CONV_PAD + tl, :] = a_ref[...] * _sigmoid(b_ref[...])
    dc = ubuf.shape[1]
    lc = 256

    def chunk(i, carry):
        r0 = pl.multiple_of(i * rc, SUBLANES)
        for c0 in range(0, dc, lc):
            win = ubuf[pl.ds(r0, rc + CONV_PAD), c0:c0 + lc]
            acc = jnp.zeros((rc, lc), F32)
            for j in range(CONV_WIDTH):
                acc = acc + win[lo + j:lo + j + rc, :] * w_ref[j:j + 1, c0:c0 + lc]
            cbuf[pl.ds(r0, rc), c0:c0 + lc] = acc
        y = cbuf[pl.ds(r0, rc), :] + bdw_ref[...]
        mu = jnp.mean(y, axis=-1, keepdims=True)
        d = y - mu
        var = jnp.mean(d * d, axis=-1, keepdims=True)
        z = d * lax.rsqrt(var + EPS) * lng_ref[...] + lnb_ref[...]
        c_ref[pl.ds(r0, rc), :] = (z * _sigmoid(z)).astype(c_ref.dtype)
        return carry

    lax.fori_loop(0, tl // rc, chunk, 0)
    tail = ubuf[tl + lo:tl + CONV_PAD, :]
    ns_ref[...] = tail
    ubuf[lo:CONV_PAD, :] = tail


def conv_branch(proj, row0, B, L, state, w_dw, b_dw, ln_g, ln_b):
    dc = w_dw.shape[1]
    tl = _pick_tile(L, (512, 256, 128, 64, 32, 16))
    rc = min(tl, 32)
    nl = L // tl
    rb0 = row0 // tl
    kern = functools.partial(_conv_kernel, tl=tl, rc=rc)
    vec = pl.BlockSpec((1, dc), lambda b, l: (0, 0))
    return pl.pallas_call(
        kern,
        out_shape=(jax.ShapeDtypeStruct((B * L, dc), BF16),
                   jax.ShapeDtypeStruct((B, CONV_WIDTH - 1, dc), F32)),
        grid=(B, nl),
        in_specs=[pl.BlockSpec((tl, dc), lambda b, l: (rb0 + b * nl + l, 0)),
                  pl.BlockSpec((tl, dc), lambda b, l: (rb0 + b * nl + l, 1)),
                  pl.BlockSpec((None, CONV_WIDTH - 1, dc), lambda b, l: (b, 0, 0)),
                  pl.BlockSpec((CONV_WIDTH, dc), lambda b, l: (0, 0)), vec, vec, vec],
        out_specs=(pl.BlockSpec((tl, dc), lambda b, l: (b * nl + l, 0)),
                   pl.BlockSpec((None, CONV_WIDTH - 1, dc), lambda b, l: (b, 0, 0))),
        scratch_shapes=[pltpu.VMEM((tl + CONV_PAD, dc), F32), pltpu.VMEM((tl, dc), F32)],
        compiler_params=_cparams("arbitrary", "arbitrary"),
        name="conv_branch",
    )(proj, proj, state, w_dw, b_dw.reshape(1, dc), ln_g.reshape(1, dc), ln_b.reshape(1, dc))


def _pair_levels(C):
    t = np.arange(C)[:, None]
    s = np.arange(C)[None, :]
    x = np.bitwise_xor(t, s)
    lvl = np.where(x > 0, np.floor(np.log2(np.maximum(x, 1))).astype(np.int32) + 1, 0)
    return np.where(s <= t, lvl, -1).astype(np.int32)


def _group_boundary(b_ref, hs, m, C):
    if 2 * m >= SUBLANES:
        pieces = [jnp.broadcast_to(b_ref[pl.ds(g * 2 * m + m - 1, 1), hs], (2 * m, LANES))
                  for g in range(C // (2 * m))]
        return pieces[0] if len(pieces) == 1 else jnp.concatenate(pieces, axis=0)
    sub = lax.broadcasted_iota(jnp.int32, (SUBLANES, LANES), 0)
    pieces = []
    for v in range(C // SUBLANES):
        piece = jnp.broadcast_to(b_ref[pl.ds(v * SUBLANES + m - 1, 1), hs], (SUBLANES, LANES))
        for j in range(1, SUBLANES // (2 * m)):
            nxt = jnp.broadcast_to(b_ref[pl.ds(v * SUBLANES + j * 2 * m + m - 1, 1), hs], (SUBLANES, LANES))
            piece = jnp.where(sub >= j * 2 * m, nxt, piece)
        pieces.append(piece)
    return jnp.concatenate(pieces, axis=0)


def _hgrn_kernel(f_ref, i_ref, q_ref, g_ref, s0_ref, loglb_ref, log1m_ref, nw_ref, tri_ref, lvl_ref,
                 o_ref, sout_ref, st_ref, lf_ref, b_ref, *, C):
    cidx = pl.program_id(1)

    @pl.when(cidx == 0)
    def _():
        def init(h, carry):
            st_ref[h] = s0_ref[h].T
            return carry
        lax.fori_loop(0, HGRN_HEADS, init, 0)

    x = f_ref[...]
    log_sig = jnp.minimum(x, 0.0) - jnp.log(1.0 + jnp.exp(-jnp.abs(x)))
    la = loglb_ref[...]
    lc = log1m_ref[...] + log_sig
    lf = jnp.maximum(la, lc) + jnp.log(1.0 + jnp.exp(-jnp.abs(la - lc)))
    lf_ref[...] = lf
    hi = lf.astype(BF16)
    lo = (lf - hi.astype(F32)).astype(BF16)
    b_ref[...] = _dot(tri_ref[...], hi) + _dot(tri_ref[...], lo)

    row = lax.broadcasted_iota(jnp.int32, (C, LANES), 0)

    def head(h, carry):
        hs = pl.ds(pl.multiple_of(h * LANES, LANES), LANES)
        lf_h = lf_ref[:, hs]
        b = b_ref[:, hs]
        q = q_ref[:, hs]
        v = i_ref[:, hs].astype(BF16)
        k = 1.0 - jnp.exp(lf_h)
        lvl = lvl_ref[...]
        p = jnp.where(lvl == 0, _dot_nt(q.astype(BF16), k.astype(BF16)), 0.0)
        m, level = 1, 1
        while m < C:
            upper = (row & m) != 0
            if m == 1:
                xpo = jnp.where(upper, lf_h, 0.0)
            else:
                bnd = _group_boundary(b_ref, hs, m, C)
                xpo = jnp.where(upper, b - bnd, bnd - b)
            ex = jnp.exp(xpo)
            pm = _dot_nt((q * ex).astype(BF16), (k * ex).astype(BF16))
            p = jnp.where(lvl == level, pm, p)
            m *= 2
            level += 1
        s_t = st_ref[h]
        o = _dot(p.astype(BF16), v) + _dot_nt((q * jnp.exp(b)).astype(BF16), s_t.astype(BF16))
        b_last = b_ref[pl.ds(C - 1, 1), hs]
        k_dec = (k * jnp.exp(b_last - b)).astype(BF16)
        st_ref[h] = s_t * jnp.exp(b_last) + _dot_tn(v, k_dec)
        ms = jnp.mean(o * o, axis=-1, keepdims=True)
        gv = g_ref[:, hs]
        o_ref[:, hs] = (o * lax.rsqrt(ms + EPS) * nw_ref[:, hs] * (gv * _sigmoid(gv))).astype(o_ref.dtype)
        return carry

    lax.fori_loop(0, HGRN_HEADS, head, 0, unroll=2)

    @pl.when(cidx == pl.num_programs(1) - 1)
    def _():
        def fin(h, carry):
            sout_ref[h] = st_ref[h].T
            return carry
        lax.fori_loop(0, HGRN_HEADS, fin, 0)


def hgrn_branch(proj, row0, B, L, s0, log_lb, log1m_lb, norm_w, col0):
    dk = HGRN_HEADS * HGRN_DK
    C = _pick_tile(L, (HGRN_CHUNK, 64, 32, 16))
    nc = L // C
    rb0 = row0 // C
    cb0 = col0 // dk
    tri = jnp.asarray(np.tril(np.ones((C, C), np.float32)), BF16)
    lvl = jnp.asarray(_pair_levels(C))
    kern = functools.partial(_hgrn_kernel, C=C)

    def pspec(j):
        return pl.BlockSpec((C, dk), functools.partial(lambda b, c, j: (rb0 + b * nc + c, cb0 + j), j=j))

    vec = pl.BlockSpec((1, dk), lambda b, c: (0, 0))
    sspec = pl.BlockSpec((None, HGRN_HEADS, HGRN_DK, HGRN_DV), lambda b, c: (b, 0, 0, 0))
    return pl.pallas_call(
        kern,
        out_shape=(jax.ShapeDtypeStruct((B * L, dk), BF16),
                   jax.ShapeDtypeStruct((B, HGRN_HEADS, HGRN_DK, HGRN_DV), F32)),
        grid=(B, nc),
        in_specs=[pspec(0), pspec(1), pspec(2), pspec(3), sspec, vec, vec, vec,
                  pl.BlockSpec((C, C), lambda b, c: (0, 0)), pl.BlockSpec((C, C), lambda b, c: (0, 0))],
        out_specs=(pl.BlockSpec((C, dk), lambda b, c: (b * nc + c, 0)), sspec),
        scratch_shapes=[pltpu.VMEM((HGRN_HEADS, HGRN_DV, HGRN_DK), F32), pltpu.VMEM((C, dk), F32),
                        pltpu.VMEM((C, dk), F32)],
        compiler_params=_cparams("arbitrary", "arbitrary"),
        name="hgrn_branch",
    )(proj, proj, proj, proj, s0, log_lb.reshape(1, dk), log1m_lb.reshape(1, dk), norm_w.reshape(1, dk), tri, lvl)


def _attn_kernel(q_ref, k_ref, v_ref, o_ref, *, scale):
    hd = q_ref.shape[1] // MEM_HEADS
    for h in range(MEM_HEADS):
        cs = slice(h * hd, (h + 1) * hd)
        s = _dot_nt(q_ref[:, cs].astype(BF16), k_ref[:, cs].astype(BF16)) * scale
        s = s - jnp.max(s, axis=-1, keepdims=True)
        e = jnp.exp(s)
        pr = e / jnp.sum(e, axis=-1, keepdims=True)
        o_ref[:, cs] = _dot(pr.astype(BF16), v_ref[:, cs].astype(BF16)).astype(o_ref.dtype)


def attn_branch(proj, row0, B, L, mem_k, mem_v, col0):
    n_mem, dm = mem_k.shape[1], mem_k.shape[2]
    tl = _pick_tile(L, (512, 256, 128, 64, 32, 16))
    nl = L // tl
    rb0 = row0 // tl
    cb0 = col0 // dm
    kern = functools.partial(_attn_kernel, scale=float((dm // MEM_HEADS) ** -0.5))
    mspec = pl.BlockSpec((None, n_mem, dm), lambda b, l: (b, 0, 0))
    return pl.pallas_call(
        kern,
        out_shape=jax.ShapeDtypeStruct((B * L, dm), BF16),
        grid=(B, nl),
        in_specs=[pl.BlockSpec((tl, dm), lambda b, l: (rb0 + b * nl + l, cb0)), mspec, mspec],
        out_specs=pl.BlockSpec((tl, dm), lambda b, l: (b * nl + l, 0)),
        compiler_params=_cparams("parallel", "parallel"),
        name="attn_branch",
    )(proj, mem_k, mem_v)


def _moe_in_kernel(be_ref, nv_ref, x_ref, wa_ref, wu_ref, o_ref, wab_ref, wub_ref):
    b = pl.program_id(1)
    fresh = jnp.logical_or(b == 0, be_ref[b] != be_ref[jnp.maximum(b - 1, 0)])

    @pl.when(fresh)
    def _():
        wab_ref[...] = wa_ref[...].astype(BF16)
        wub_ref[...] = wu_ref[...].astype(BF16)

    @pl.when(b < nv_ref[0])
    def _():
        x = _unpack_pairs(x_ref[...])
        a = _dot(x, wab_ref[...])
        u = _dot(x, wub_ref[...])
        o_ref[...] = (a * _sigmoid(a) * u).astype(o_ref.dtype)

    @pl.when(b >= nv_ref[0])
    def _():
        o_ref[...] = jnp.zeros_like(o_ref)


def _moe_out_kernel(be_ref, nv_ref, h_ref, w_ref, g_ref, o_ref, wb_ref):
    b = pl.program_id(1)
    fresh = jnp.logical_or(b == 0, be_ref[b] != be_ref[jnp.maximum(b - 1, 0)])

    @pl.when(fresh)
    def _():
        wb_ref[...] = w_ref[...].astype(BF16)

    @pl.when(b < nv_ref[0])
    def _():
        o_ref[...] = _dot(h_ref[...], wb_ref[...]) * g_ref[...]

    @pl.when(b >= nv_ref[0])
    def _():
        o_ref[...] = jnp.zeros_like(o_ref)


def moe_experts(xs, blk_e, n_valid, buf_gate, w_ein, w_eout, layer):
    P = xs.shape[0]
    D = w_ein.shape[2]
    de = w_eout.shape[2]
    tm = MOE_TM
    nb = P // tm
    tf = 512
    nj = de // tf
    hmid = pl.pallas_call(
        _moe_in_kernel,
        out_shape=jax.ShapeDtypeStruct((P, de), BF16),
        grid_spec=pltpu.PrefetchScalarGridSpec(
            num_scalar_prefetch=2, grid=(nj, nb),
            in_specs=[pl.BlockSpec((tm, D // 2), lambda j, b, be, nv: (jnp.minimum(b, nv[0] - 1), 0)),
                      pl.BlockSpec((None, None, D, tf), lambda j, b, be, nv: (layer, be[b], 0, j)),
                      pl.BlockSpec((None, None, D, tf), lambda j, b, be, nv: (layer, be[b], 0, nj + j))],
            out_specs=pl.BlockSpec((tm, tf), lambda j, b, be, nv: (b, j)),
            scratch_shapes=[pltpu.VMEM((D, tf), BF16), pltpu.VMEM((D, tf), BF16)]),
        compiler_params=_cparams("arbitrary", "arbitrary"),
        name="moe_in",
    )(blk_e, n_valid, xs, w_ein, w_ein)
    tn = 2048
    return pl.pallas_call(
        _moe_out_kernel,
        out_shape=jax.ShapeDtypeStruct((P, D), F32),
        grid_spec=pltpu.PrefetchScalarGridSpec(
            num_scalar_prefetch=2, grid=(D // tn, nb),
            in_specs=[pl.BlockSpec((tm, de), lambda n, b, be, nv: (jnp.minimum(b, nv[0] - 1), 0)),
                      pl.BlockSpec((None, None, de, tn), lambda n, b, be, nv: (layer, be[b], 0, n)),
                      pl.BlockSpec((tm, 1), lambda n, b, be, nv: (b, 0))],
            out_specs=pl.BlockSpec((tm, tn), lambda n, b, be, nv: (b, n)),
            scratch_shapes=[pltpu.VMEM((de, tn), BF16)]),
        compiler_params=_cparams("arbitrary", "arbitrary"),
        name="moe_out",
    )(blk_e, n_valid, hmid, w_eout, buf_gate.reshape(P, 1))


def _route(logits, b_rg, b_re):
    T = logits.shape[0]
    g_logits = logits[:, :N_GROUPS] + b_rg
    g_prob = jax.nn.softmax(g_logits, axis=-1)
    grp = jnp.argmax(g_logits, axis=-1).astype(jnp.int32)
    p_grp = jnp.take_along_axis(g_prob, grp[:, None], axis=-1)
    e_logits = (logits[:, N_GROUPS:N_GROUPS + N_EXPERTS] + b_re).reshape(T, N_GROUPS, EXPERTS_PER_GROUP)
    e_sel = jnp.take_along_axis(e_logits, grp[:, None, None], axis=1)[:, 0]
    top_val, top_idx = lax.top_k(e_sel, TOP_K)
    gate = p_grp * jax.nn.softmax(top_val, axis=-1)
    eid = grp[:, None] * EXPERTS_PER_GROUP + top_idx
    return eid, gate


def hier_moe(x, norm_w, w_rg, b_rg, w_re, b_re, w_ein, w_eout, layer):
    T, D = x.shape
    tm = MOE_TM
    w_router = jnp.concatenate([w_rg, w_re, jnp.zeros((D, LANES - N_GROUPS - N_EXPERTS), F32)], axis=1)
    h, logits = rmsnorm_router(x, norm_w, w_router)
    eid, gate = _route(logits, b_rg, b_re)
    A = T * TOP_K
    flat_e = eid.reshape(-1)
    onehot = (flat_e[:, None] == jnp.arange(N_EXPERTS, dtype=jnp.int32)[None, :]).astype(jnp.int32)
    csum = jnp.cumsum(onehot, axis=0)
    rank = jnp.take_along_axis(csum, flat_e[:, None], axis=1)[:, 0] - 1
    counts = csum[-1]
    padded = (counts + tm - 1) // tm * tm
    pend = jnp.cumsum(padded)
    pstart = pend - padded
    dest = pstart[flat_e] + rank
    nb = -(-(A + N_EXPERTS * (tm - 1)) // tm)
    P = nb * tm
    flat_tok = jnp.repeat(jnp.arange(T, dtype=jnp.int32), TOP_K)
    buf_tok = jnp.zeros((P,), jnp.int32).at[dest].set(flat_tok)
    buf_gate = jnp.zeros((P,), F32).at[dest].set(gate.reshape(-1))
    n_valid = (pend[-1] // tm).astype(jnp.int32)
    blk_raw = jnp.minimum(jnp.sum(jnp.arange(nb, dtype=jnp.int32)[:, None] * tm >= pend[None, :], axis=1),
                          N_EXPERTS - 1).astype(jnp.int32)
    blk_e = jnp.where(jnp.arange(nb) < n_valid, blk_raw, blk_raw[jnp.maximum(n_valid - 1, 0)])
    xs = jnp.take(h, buf_tok, axis=0, mode="clip")
    y = moe_experts(xs, blk_e, n_valid.reshape(1), buf_gate, w_ein, w_eout, layer)
    dest2 = dest.reshape(T, TOP_K)
    return add3(x, jnp.take(y, dest2[:, 0], axis=0, mode="clip"), jnp.take(y, dest2[:, 1], axis=0, mode="clip"))


def _layer(x, groups, log_lb, log1m_lb, p, layer):
    dc = p['w_dw'].shape[1]
    dk = HGRN_HEADS * HGRN_DK
    dm = p['w_mem_out'].shape[1]
    D = x.shape[1]
    h = rmsnorm(x, p['norm_mix'], BF16)
    proj = dense([h], [p['w_in']], layer, F32, name="in_proj", tn=1024, tm_max=320)
    col_hgrn = 2 * dc
    col_mem = col_hgrn + 4 * dk
    col_gate = col_mem + dm
    cs, os_, oms, conv_states, hgrn_states = [], [], [], [], []
    for row0, B, L, conv_state, s0, mem_k, mem_v in groups:
        c, ns = conv_branch(proj, row0, B, L, conv_state, p['w_dw'], p['b_dw'], p['conv_ln_g'], p['conv_ln_b'])
        o, s_new = hgrn_branch(proj, row0, B, L, s0, log_lb, log1m_lb, p['hgrn_norm'], col_hgrn)
        om = attn_branch(proj, row0, B, L, mem_k.reshape(B, -1, dm), mem_v.reshape(B, -1, dm), col_mem)
        cs.append(c); os_.append(o); oms.append(om); conv_states.append(ns); hgrn_states.append(s_new)
    cat = lambda xs: xs[0] if len(xs) == 1 else jnp.concatenate(xs, axis=0)
    merged = dense([cat(cs), cat(os_), cat(oms)], [p['w_conv_out'], p['w_hgrn_out'], p['w_mem_out']], layer, BF16,
                   gate_src=proj, gate_cols=[col_gate, col_gate + D, col_gate + 2 * D], name="branch_merge")
    x = dense([merged], [p['w_out']], layer, F32, residual=x, name="out_proj", tn=1024, tm_max=320)
    x = hier_moe(x, p['norm_ffn'], p['w_router_group'], p['b_router_group'], p['w_router_expert'],
                 p['b_router_expert'], p['w_exp_in'], p['w_exp_out'], layer)
    return x, conv_states, hgrn_states


def kernel(x_prompt, x_sample, state_conv, state_hgrn, cache_mem_k, cache_mem_v, mem_prompt, norm_mix, norm_mem, norm_ffn, norm_final, w_in, w_dw, b_dw, conv_ln_g, conv_ln_b, w_conv_out, hgrn_lb, hgrn_norm, w_hgrn_out, w_mem_kv, w_mem_out, w_out, w_router_group, b_router_group, w_router_expert, b_router_expert, w_exp_in, w_exp_out):
    depth = w_in.shape[0]
    Bp, Lp, D = x_prompt.shape
    Bs, Ls, _ = x_sample.shape
    n_mem = mem_prompt.shape[1]
    dc = w_dw.shape[2]
    dm = w_mem_out.shape[1]
    lb_all = jnp.cumsum(jax.nn.softmax(hgrn_lb.astype(F32), axis=0), axis=0)
    lb_all = lb_all - lb_all[:1]
    log_lb = jnp.log(lb_all)
    log1m_lb = jnp.log1p(-lb_all)
    x = jnp.concatenate([x_prompt.reshape(Bp * Lp, D), x_sample.reshape(Bs * Ls, D)], axis=0)
    conv0 = jnp.zeros((Bp, CONV_WIDTH - 1, dc), F32)
    s0 = jnp.zeros((Bp, HGRN_HEADS, HGRN_DK, HGRN_DV), F32)
    mem_flat = mem_prompt.reshape(Bp * n_mem, D)
    conv_p, hgrn_p, mk_p, mv_p, conv_s, hgrn_s = [], [], [], [], [], []
    for l in range(depth):
        p = dict(norm_mix=norm_mix[l], w_in=w_in, w_dw=w_dw[l], b_dw=b_dw[l], conv_ln_g=conv_ln_g[l],
                 conv_ln_b=conv_ln_b[l], w_conv_out=w_conv_out, hgrn_norm=hgrn_norm[l], w_hgrn_out=w_hgrn_out,
                 w_mem_out=w_mem_out, w_out=w_out, norm_ffn=norm_ffn[l], w_router_group=w_router_group[l],
                 b_router_group=b_router_group[l], w_router_expert=w_router_expert[l],
                 b_router_expert=b_router_expert[l], w_exp_in=w_exp_in, w_exp_out=w_exp_out)
        kv = dense([rmsnorm(mem_flat, norm_mem[l], BF16)], [w_mem_kv], l, F32, name="mem_kv")
        mk = kv[:, :dm].reshape(Bp, n_mem, MEM_HEADS, dm // MEM_HEADS)
        mv = kv[:, dm:].reshape(Bp, n_mem, MEM_HEADS, dm // MEM_HEADS)
        groups = [(0, Bp, Lp, conv0, s0, mk, mv),
                  (Bp * Lp, Bs, Ls, state_conv[l], state_hgrn[l], cache_mem_k[l], cache_mem_v[l])]
        x, cstates, hstates = _layer(x, groups, log_lb[l], log1m_lb[l], p, l)
        conv_p.append(cstates[0]); hgrn_p.append(hstates[0]); mk_p.append(mk); mv_p.append(mv)
        conv_s.append(cstates[1]); hgrn_s.append(hstates[1])
    y = rmsnorm(x, norm_final, F32)
    y_prompt = y[:Bp * Lp].reshape(Bp, Lp, D)
    y_sample = y[Bp * Lp:].reshape(Bs, Ls, D)
    return (y_prompt, y_sample, jnp.stack(conv_p), jnp.stack(hgrn_p), jnp.stack(mk_p), jnp.stack(mv_p),
            jnp.stack(conv_s), jnp.stack(hgrn_s))
```

```python
import functools

import numpy as np
import jax
import jax.numpy as jnp
from jax import lax
from jax.experimental import pallas as pl
from jax.experimental.pallas import tpu as pltpu

F32 = jnp.float32
BF16 = jnp.bfloat16

EPS = 1e-6
CONV_WIDTH = 31
HGRN_HEADS = 16
HGRN_DK = 128
HGRN_DV = 128
MEM_HEADS = 4
N_GROUPS = 4
EXPERTS_PER_GROUP = 8
N_EXPERTS = N_GROUPS * EXPERTS_PER_GROUP
TOP_K = 2

LANES = 128
SUBLANES = 8
VMEM_LIMIT_BYTES = 56 * 1024 * 1024
MM_TN = 512
MOE_ROWS = 640
MOE_SUB = 128
CONV_PAD = 32
HGRN_CHUNK = 128


def _cparams(*sem):
    return pltpu.CompilerParams(dimension_semantics=sem, vmem_limit_bytes=VMEM_LIMIT_BYTES)


def _pick_tile(n, cands):
    for c in cands:
        if n % c == 0:
            return c
    return n


def _sigmoid(x):
    return 1.0 / (1.0 + jnp.exp(-x))


def _dot(a, b):
    return jnp.dot(a, b, preferred_element_type=F32)


def _dot_nt(a, b):
    return lax.dot_general(a, b, (((1,), (1,)), ((), ())), preferred_element_type=F32)


def _dot_tn(a, b):
    return lax.dot_general(a, b, (((0,), (0,)), ((), ())), preferred_element_type=F32)


def _rms_body(x, g):
    ms = jnp.mean(x * x, axis=-1, keepdims=True)
    return x * lax.rsqrt(ms + EPS) * g


def _rms_kernel(x_ref, g_ref, o_ref):
    o_ref[...] = _rms_body(x_ref[...], g_ref[...]).astype(o_ref.dtype)


def rmsnorm(x, g, out_dtype):
    T, D = x.shape
    tm = _pick_tile(T, (320, 256, 128, 64, 32, 16, 8))
    return pl.pallas_call(
        _rms_kernel,
        out_shape=jax.ShapeDtypeStruct((T, D), out_dtype),
        grid=(T // tm,),
        in_specs=[pl.BlockSpec((tm, D), lambda i: (i, 0)), pl.BlockSpec((1, D), lambda i: (0, 0))],
        out_specs=pl.BlockSpec((tm, D), lambda i: (i, 0)),
        compiler_params=_cparams("parallel"),
        name="rmsnorm",
    )(x, g.reshape(1, D))


def _pack_pairs(h):
    half = h.shape[1] // 2
    bits = lax.bitcast_convert_type(h.astype(BF16).astype(F32), jnp.uint32)
    return bits[:, :half] | (bits[:, half:] >> 16)


def _unpack_pairs(w):
    hi = lax.bitcast_convert_type(w & jnp.uint32(0xFFFF0000), F32).astype(BF16)
    lo = lax.bitcast_convert_type(w << 16, F32).astype(BF16)
    return jnp.concatenate([hi, lo], axis=1)


def _rms_router_kernel(x_ref, g_ref, wr_ref, h_ref, lg_ref):
    h = _rms_body(x_ref[...], g_ref[...])
    h_ref[...] = _pack_pairs(h)
    lg_ref[...] = jnp.dot(h, wr_ref[...], precision=lax.Precision.HIGHEST, preferred_element_type=F32)


def rmsnorm_router(x, g, w_router):
    T, D = x.shape
    R = w_router.shape[1]
    tm = _pick_tile(T, (320, 256, 128, 64, 32, 16, 8))
    return pl.pallas_call(
        _rms_router_kernel,
        out_shape=(jax.ShapeDtypeStruct((T, D // 2), jnp.uint32), jax.ShapeDtypeStruct((T, R), F32)),
        grid=(T // tm,),
        in_specs=[pl.BlockSpec((tm, D), lambda i: (i, 0)), pl.BlockSpec((1, D), lambda i: (0, 0)),
                  pl.BlockSpec((D, R), lambda i: (0, 0))],
        out_specs=(pl.BlockSpec((tm, D // 2), lambda i: (i, 0)), pl.BlockSpec((tm, R), lambda i: (i, 0))),
        compiler_params=_cparams("parallel"),
        name="rmsnorm_router",
    )(x, g.reshape(1, D), w_router)


def _add3_kernel(x_ref, a_ref, b_ref, o_ref):
    o_ref[...] = x_ref[...] + (a_ref[...] + b_ref[...])


def add3(x, a, b):
    T, D = x.shape
    tm = _pick_tile(T, (320, 256, 128, 64, 32, 16, 8))
    spec = pl.BlockSpec((tm, D), lambda i: (i, 0))
    return pl.pallas_call(
        _add3_kernel, out_shape=jax.ShapeDtypeStruct((T, D), F32), grid=(T // tm,),
        in_specs=[spec, spec, spec], out_specs=spec, compiler_params=_cparams("parallel"), name="moe_combine",
    )(x, a, b)


def _mm_kernel(*refs, n_terms, gated, residual):
    a_refs = refs[:n_terms]
    w_refs = refs[n_terms:2 * n_terms]
    pos = 2 * n_terms
    g_refs = refs[pos:pos + n_terms] if gated else ()
    pos += n_terms if gated else 0
    r_ref = refs[pos] if residual else None
    pos += 1 if residual else 0
    o_ref = refs[pos]
    wb_refs = refs[pos + 1:pos + 1 + n_terms]

    @pl.when(pl.program_id(1) == 0)
    def _():
        for w_ref, wb_ref in zip(w_refs, wb_refs):
            wb_ref[...] = w_ref[...].astype(BF16)

    acc = None
    for i in range(n_terms):
        y = _dot(a_refs[i][...], wb_refs[i][...])
        if gated:
            y = y * _sigmoid(g_refs[i][...])
        acc = y if acc is None else acc + y
    if residual:
        acc = acc + r_ref[...]
    o_ref[...] = acc.astype(o_ref.dtype)


def dense(a_list, w_list, layer, out_dtype, gate_src=None, gate_cols=None, residual=None, name="dense",
          tn=MM_TN, tm_max=640):
    n_terms = len(a_list)
    T = a_list[0].shape[0]
    N = w_list[0].shape[2]
    tm = _pick_tile(T, tuple(c for c in (640, 512, 320, 256, 128, 64, 32, 16, 8) if c <= tm_max))
    in_specs, args = [], []
    for a in a_list:
        in_specs.append(pl.BlockSpec((tm, a.shape[1]), lambda n, m: (m, 0)))
        args.append(a)
    for w in w_list:
        in_specs.append(pl.BlockSpec((None, w.shape[1], tn), lambda n, m: (layer, 0, n)))
        args.append(w)
    if gate_src is not None:
        for c0 in gate_cols:
            in_specs.append(pl.BlockSpec((tm, tn), functools.partial(lambda n, m, cb: (m, cb + n), cb=c0 // tn)))
            args.append(gate_src)
    if residual is not None:
        in_specs.append(pl.BlockSpec((tm, tn), lambda n, m: (m, n)))
        args.append(residual)
    kern = functools.partial(_mm_kernel, n_terms=n_terms, gated=gate_src is not None,
                             residual=residual is not None)
    return pl.pallas_call(
        kern,
        out_shape=jax.ShapeDtypeStruct((T, N), out_dtype),
        grid=(N // tn, T // tm),
        in_specs=in_specs,
        out_specs=pl.BlockSpec((tm, tn), lambda n, m: (m, n)),
        scratch_shapes=[pltpu.VMEM((w.shape[1], tn), BF16) for w in w_list],
        compiler_params=_cparams("arbitrary", "arbitrary"),
        name=name,
    )(*args)


def _conv_kernel(a_ref, b_ref, st_ref, w_ref, bdw_ref, lng_ref, lnb_ref, c_ref, ns_ref, ubuf, sh, cbuf, *, tl, rc):
    W1 = CONV_WIDTH - 1

    @pl.when(pl.program_id(1) == 0)
    def _():
        ubuf[0:W1, :] = st_ref[...]

    ubuf[W1:W1 + tl, :] = a_ref[...] * _sigmoid(b_ref[...])
    for r in range(1, SUBLANES):
        span = tl + SUBLANES * ((W1 - r) // SUBLANES)
        sh[r - 1, 0:span, :] = ubuf[r:r + span, :]
    dc = ubuf.shape[1]
    lc = LANES

    def chunk(i, carry):
        r0 = pl.multiple_of(i * rc, SUBLANES)
        for c0 in range(0, dc, lc):
            acc = jnp.zeros((rc, lc), F32)
            for j in range(CONV_WIDTH):
                q, r = divmod(j, SUBLANES)
                src = ubuf if r == 0 else sh.at[r - 1]
                acc = acc + src[pl.ds(r0 + SUBLANES * q, rc), c0:c0 + lc] * w_ref[j:j + 1, c0:c0 + lc]
            cbuf[pl.ds(r0, rc), c0:c0 + lc] = acc
        y = cbuf[pl.ds(r0, rc), :] + bdw_ref[...]
        mu = jnp.mean(y, axis=-1, keepdims=True)
        d = y - mu
        var = jnp.mean(d * d, axis=-1, keepdims=True)
        z = d * lax.rsqrt(var + EPS) * lng_ref[...] + lnb_ref[...]
        c_ref[pl.ds(r0, rc), :] = (z * _sigmoid(z)).astype(c_ref.dtype)
        return carry

    lax.fori_loop(0, tl // rc, chunk, 0)
    tail = ubuf[tl:tl + W1, :]
    ns_ref[...] = tail
    ubuf[0:W1, :] = tail


def conv_branch(proj, row0, B, L, state, w_dw, b_dw, ln_g, ln_b):
    dc = w_dw.shape[1]
    tl = _pick_tile(L, (512, 256, 128, 64, 32, 16))
    rc = min(tl, 64)
    nl = L // tl
    rb0 = row0 // tl
    kern = functools.partial(_conv_kernel, tl=tl, rc=rc)
    vec = pl.BlockSpec((1, dc), lambda b, l: (0, 0))
    return pl.pallas_call(
        kern,
        out_shape=(jax.ShapeDtypeStruct((B * L, dc), BF16),
                   jax.ShapeDtypeStruct((B, CONV_WIDTH - 1, dc), F32)),
        grid=(B, nl),
        in_specs=[pl.BlockSpec((tl, dc), lambda b, l: (rb0 + b * nl + l, 0)),
                  pl.BlockSpec((tl, dc), lambda b, l: (rb0 + b * nl + l, 1)),
                  pl.BlockSpec((None, CONV_WIDTH - 1, dc), lambda b, l: (b, 0, 0)),
                  pl.BlockSpec((CONV_WIDTH, dc), lambda b, l: (0, 0)), vec, vec, vec],
        out_specs=(pl.BlockSpec((tl, dc), lambda b, l: (b * nl + l, 0)),
                   pl.BlockSpec((None, CONV_WIDTH - 1, dc), lambda b, l: (b, 0, 0))),
        scratch_shapes=[pltpu.VMEM((tl + CONV_PAD, dc), F32),
                        pltpu.VMEM((SUBLANES - 1, tl + CONV_PAD - SUBLANES, dc), F32),
                        pltpu.VMEM((tl, dc), F32)],
        compiler_params=_cparams("arbitrary", "arbitrary"),
        name="conv_branch",
    )(proj, proj, state, w_dw, b_dw.reshape(1, dc), ln_g.reshape(1, dc), ln_b.reshape(1, dc))


def _pair_levels(C):
    t = np.arange(C)[:, None]
    s = np.arange(C)[None, :]
    x = np.bitwise_xor(t, s)
    lvl = np.where(x > 0, np.floor(np.log2(np.maximum(x, 1))).astype(np.int32) + 1, 0)
    return np.where(s <= t, lvl, -1).astype(np.int32)


def _group_boundary(b_ref, hs, m, C):
    if 2 * m >= SUBLANES:
        pieces = [jnp.broadcast_to(b_ref[pl.ds(g * 2 * m + m - 1, 1), hs], (2 * m, LANES))
                  for g in range(C // (2 * m))]
        return pieces[0] if len(pieces) == 1 else jnp.concatenate(pieces, axis=0)
    sub = lax.broadcasted_iota(jnp.int32, (SUBLANES, LANES), 0)
    pieces = []
    for v in range(C // SUBLANES):
        piece = jnp.broadcast_to(b_ref[pl.ds(v * SUBLANES + m - 1, 1), hs], (SUBLANES, LANES))
        for j in range(1, SUBLANES // (2 * m)):
            nxt = jnp.broadcast_to(b_ref[pl.ds(v * SUBLANES + j * 2 * m + m - 1, 1), hs], (SUBLANES, LANES))
            piece = jnp.where(sub >= j * 2 * m, nxt, piece)
        pieces.append(piece)
    return jnp.concatenate(pieces, axis=0)


def _hgrn_kernel(f_ref, i_ref, q_ref, g_ref, s0_ref, loglb_ref, log1m_ref, nw_ref, tri_ref, lvl_ref,
                 o_ref, sout_ref, st_ref, lf_ref, b_ref, *, C):
    cidx = pl.program_id(1)

    @pl.when(cidx == 0)
    def _():
        def init(h, carry):
            st_ref[h] = s0_ref[h].T
            return carry
        lax.fori_loop(0, HGRN_HEADS, init, 0)

    x = f_ref[...]
    log_sig = jnp.minimum(x, 0.0) - jnp.log(1.0 + jnp.exp(-jnp.abs(x)))
    la = loglb_ref[...]
    lc = log1m_ref[...] + log_sig
    lf = jnp.maximum(la, lc) + jnp.log(1.0 + jnp.exp(-jnp.abs(la - lc)))
    lf_ref[...] = lf
    hi = lf.astype(BF16)
    lo = (lf - hi.astype(F32)).astype(BF16)
    b_ref[...] = _dot(tri_ref[...], hi) + _dot(tri_ref[...], lo)

    row = lax.broadcasted_iota(jnp.int32, (C, LANES), 0)

    def head(h, carry):
        hs = pl.ds(pl.multiple_of(h * LANES, LANES), LANES)
        lf_h = lf_ref[:, hs]
        b = b_ref[:, hs]
        q = q_ref[:, hs]
        v = i_ref[:, hs].astype(BF16)
        k = 1.0 - jnp.exp(lf_h)
        lvl = lvl_ref[...]
        p = jnp.where(lvl == 0, _dot_nt(q.astype(BF16), k.astype(BF16)), 0.0)
        m, level = 1, 1
        while m < C:
            upper = (row & m) != 0
            if m == 1:
                xpo = jnp.where(upper, lf_h, 0.0)
            else:
                bnd = _group_boundary(b_ref, hs, m, C)
                xpo = jnp.where(upper, b - bnd, bnd - b)
            ex = jnp.exp(xpo)
            pm = _dot_nt((q * ex).astype(BF16), (k * ex).astype(BF16))
            p = jnp.where(lvl == level, pm, p)
            m *= 2
            level += 1
        s_t = st_ref[h]
        o = _dot(p.astype(BF16), v) + _dot_nt((q * jnp.exp(b)).astype(BF16), s_t.astype(BF16))
        b_last = b_ref[pl.ds(C - 1, 1), hs]
        k_dec = (k * jnp.exp(b_last - b)).astype(BF16)
        st_ref[h] = s_t * jnp.exp(b_last) + _dot_tn(v, k_dec)
        ms = jnp.mean(o * o, axis=-1, keepdims=True)
        gv = g_ref[:, hs]
        o_ref[:, hs] = (o * lax.rsqrt(ms + EPS) * nw_ref[:, hs] * (gv * _sigmoid(gv))).astype(o_ref.dtype)
        return carry

    lax.fori_loop(0, HGRN_HEADS, head, 0, unroll=2)

    @pl.when(cidx == pl.num_programs(1) - 1)
    def _():
        def fin(h, carry):
            sout_ref[h] = st_ref[h].T
            return carry
        lax.fori_loop(0, HGRN_HEADS, fin, 0)


def hgrn_branch(proj, row0, B, L, s0, log_lb, log1m_lb, norm_w, col0):
    dk = HGRN_HEADS * HGRN_DK
    C = _pick_tile(L, (HGRN_CHUNK, 64, 32, 16))
    nc = L // C
    rb0 = row0 // C
    cb0 = col0 // dk
    tri = jnp.asarray(np.tril(np.ones((C, C), np.float32)), BF16)
    lvl = jnp.asarray(_pair_levels(C))
    kern = functools.partial(_hgrn_kernel, C=C)

    def pspec(j):
        return pl.BlockSpec((C, dk), functools.partial(lambda b, c, j: (rb0 + b * nc + c, cb0 + j), j=j))

    vec = pl.BlockSpec((1, dk), lambda b, c: (0, 0))
    sspec = pl.BlockSpec((None, HGRN_HEADS, HGRN_DK, HGRN_DV), lambda b, c: (b, 0, 0, 0))
    return pl.pallas_call(
        kern,
        out_shape=(jax.ShapeDtypeStruct((B * L, dk), BF16),
                   jax.ShapeDtypeStruct((B, HGRN_HEADS, HGRN_DK, HGRN_DV), F32)),
        grid=(B, nc),
        in_specs=[pspec(0), pspec(1), pspec(2), pspec(3), sspec, vec, vec, vec,
                  pl.BlockSpec((C, C), lambda b, c: (0, 0)), pl.BlockSpec((C, C), lambda b, c: (0, 0))],
        out_specs=(pl.BlockSpec((C, dk), lambda b, c: (b * nc + c, 0)), sspec),
        scratch_shapes=[pltpu.VMEM((HGRN_HEADS, HGRN_DV, HGRN_DK), F32), pltpu.VMEM((C, dk), F32),
                        pltpu.VMEM((C, dk), F32)],
        compiler_params=_cparams("arbitrary", "arbitrary"),
        name="hgrn_branch",
    )(proj, proj, proj, proj, s0, log_lb.reshape(1, dk), log1m_lb.reshape(1, dk), norm_w.reshape(1, dk), tri, lvl)


def _attn_kernel(q_ref, k_ref, v_ref, o_ref, *, scale):
    hd = q_ref.shape[1] // MEM_HEADS
    for h in range(MEM_HEADS):
        cs = slice(h * hd, (h + 1) * hd)
        s = _dot_nt(q_ref[:, cs].astype(BF16), k_ref[:, cs].astype(BF16)) * scale
        s = s - jnp.max(s, axis=-1, keepdims=True)
        e = jnp.exp(s)
        pr = e / jnp.sum(e, axis=-1, keepdims=True)
        o_ref[:, cs] = _dot(pr.astype(BF16), v_ref[:, cs].astype(BF16)).astype(o_ref.dtype)


def attn_branch(proj, row0, B, L, mem_k, mem_v, col0):
    n_mem, dm = mem_k.shape[1], mem_k.shape[2]
    tl = _pick_tile(L, (512, 256, 128, 64, 32, 16))
    nl = L // tl
    rb0 = row0 // tl
    cb0 = col0 // dm
    kern = functools.partial(_attn_kernel, scale=float((dm // MEM_HEADS) ** -0.5))
    mspec = pl.BlockSpec((None, n_mem, dm), lambda b, l: (b, 0, 0))
    return pl.pallas_call(
        kern,
        out_shape=jax.ShapeDtypeStruct((B * L, dm), BF16),
        grid=(B, nl),
        in_specs=[pl.BlockSpec((tl, dm), lambda b, l: (rb0 + b * nl + l, cb0)), mspec, mspec],
        out_specs=pl.BlockSpec((tl, dm), lambda b, l: (b * nl + l, 0)),
        compiler_params=_cparams("parallel", "parallel"),
        name="attn_branch",
    )(proj, mem_k, mem_v)


def _moe_kernel(be_ref, nr_ref, nv_ref, x_ref, win_ref, wout_ref, g_ref, o_ref, acc_ref, hm_ref, wb_ref, *, n1, sb):
    b = pl.program_id(0)
    s = pl.program_id(1)
    nr = nr_ref[b]
    R, kc = x_ref.shape[0], win_ref.shape[0]
    F, nc = wout_ref.shape
    nh = n1 // 2

    nsub = (nr + sb - 1) // sb

    @pl.when(s < n1)
    def _():
        wb_ref[...] = win_ref[...].astype(BF16)
        col0 = pl.multiple_of((s % nh) * kc, kc)
        shift = ((s // nh) * 16).astype(jnp.uint32)
        for k in range(1, R // sb + 1):
            rows = slice(0, k * sb)

            @pl.when(nsub == k)
            def _():
                xw = x_ref[rows, pl.ds(col0, kc)]
                xk = lax.bitcast_convert_type((xw << shift) & jnp.uint32(0xFFFF0000), F32).astype(BF16)
                part = _dot(xk, wb_ref[...])

                @pl.when(s == 0)
                def _():
                    acc_ref[rows, :] = part

                @pl.when(s > 0)
                def _():
                    acc_ref[rows, :] += part

                @pl.when(s == n1 - 1)
                def _():
                    a = acc_ref[rows, :F]
                    hm_ref[rows, :] = (a * _sigmoid(a) * acc_ref[rows, F:]).astype(BF16)

    @pl.when(s >= n1)
    def _():
        wb_ref[:, :nc] = wout_ref[...].astype(BF16)
        for k in range(0, R // sb + 1):
            @pl.when(nsub == k)
            def _():
                if k > 0:
                    rows = slice(0, k * sb)
                    o_ref[rows, :] = _dot(hm_ref[rows, :], wb_ref[:, :nc]) * g_ref[rows, :]
                if k < R // sb:
                    o_ref[k * sb:, :] = jnp.zeros((R - k * sb, nc), F32)


def moe_experts(xs, blk_e, blk_rows, n_valid, buf_gate, w_ein, w_eout, layer):
    P = xs.shape[0]
    D, F2 = w_ein.shape[2], w_ein.shape[3]
    F = w_eout.shape[2]
    R = MOE_ROWS
    nb = P // R
    n1 = 4
    kc = D // n1
    nc = 1024
    n2 = D // nc

    def in_chunk(b, s, be, nr, nv):
        return (layer, be[b], jnp.where(b < nv[0], jnp.minimum(s, n1 - 1), n1 - 1), 0)

    def out_chunk(b, s, be, nr, nv):
        return (layer, be[b], 0, jnp.where(b < nv[0], jnp.maximum(s - n1, 0), n2 - 1))

    return pl.pallas_call(
        functools.partial(_moe_kernel, n1=n1, sb=MOE_SUB),
        out_shape=jax.ShapeDtypeStruct((P, D), F32),
        grid_spec=pltpu.PrefetchScalarGridSpec(
            num_scalar_prefetch=3, grid=(nb, n1 + n2),
            in_specs=[pl.BlockSpec((R, D // 2), lambda b, s, be, nr, nv: (jnp.minimum(b, nv[0] - 1), 0)),
                      pl.BlockSpec((None, None, kc, F2), in_chunk),
                      pl.BlockSpec((None, None, F, nc), out_chunk),
                      pl.BlockSpec((R, 1), lambda b, s, be, nr, nv: (b, 0))],
            out_specs=pl.BlockSpec((R, nc), lambda b, s, be, nr, nv: (b, jnp.maximum(s - n1, 0))),
            scratch_shapes=[pltpu.VMEM((R, F2), F32), pltpu.VMEM((R, F), BF16), pltpu.VMEM((kc, F2), BF16)]),
        compiler_params=_cparams("arbitrary", "arbitrary"),
        name="moe_experts",
    )(blk_e, blk_rows, n_valid, xs, w_ein, w_eout, buf_gate.reshape(P, 1))


def _route(logits, b_rg, b_re):
    T = logits.shape[0]
    g_logits = logits[:, :N_GROUPS] + b_rg
    g_prob = jax.nn.softmax(g_logits, axis=-1)
    grp = jnp.argmax(g_logits, axis=-1).astype(jnp.int32)
    p_grp = jnp.take_along_axis(g_prob, grp[:, None], axis=-1)
    e_logits = (logits[:, N_GROUPS:N_GROUPS + N_EXPERTS] + b_re).reshape(T, N_GROUPS, EXPERTS_PER_GROUP)
    e_sel = jnp.take_along_axis(e_logits, grp[:, None, None], axis=1)[:, 0]
    top_val, top_idx = lax.top_k(e_sel, TOP_K)
    gate = p_grp * jax.nn.softmax(top_val, axis=-1)
    eid = grp[:, None] * EXPERTS_PER_GROUP + top_idx
    return eid, gate


def hier_moe(x, norm_w, w_rg, b_rg, w_re, b_re, w_ein, w_eout, layer):
    T, D = x.shape
    w_router = jnp.concatenate([w_rg, w_re, jnp.zeros((D, LANES - N_GROUPS - N_EXPERTS), F32)], axis=1)
    h, logits = rmsnorm_router(x, norm_w, w_router)
    eid, gate = _route(logits, b_rg, b_re)
    R = MOE_ROWS
    A = T * TOP_K
    flat_e = eid.reshape(-1)
    onehot = (flat_e[:, None] == jnp.arange(N_EXPERTS, dtype=jnp.int32)[None, :]).astype(jnp.int32)
    csum = jnp.cumsum(onehot, axis=0)
    rank = jnp.take_along_axis(csum, flat_e[:, None], axis=1)[:, 0] - 1
    counts = csum[-1]
    padded = (counts + R - 1) // R * R
    pend = jnp.cumsum(padded)
    pstart = pend - padded
    dest = pstart[flat_e] + rank
    nb = -(-(A + N_EXPERTS * (R - 1)) // R)
    P = nb * R
    flat_tok = jnp.repeat(jnp.arange(T, dtype=jnp.int32), TOP_K)
    slot_vals = jnp.stack([flat_tok, lax.bitcast_convert_type(gate.reshape(-1), jnp.int32)], axis=1)
    spread = jnp.stack([jnp.arange(P, dtype=jnp.int32) % T, jnp.zeros((P,), jnp.int32)], axis=1)
    buf = spread.at[dest].set(slot_vals)
    buf_tok = buf[:, 0]
    buf_gate = lax.bitcast_convert_type(buf[:, 1], F32)
    blk = jnp.arange(nb, dtype=jnp.int32)
    n_valid = (pend[-1] // R).astype(jnp.int32)
    blk_raw = jnp.minimum(jnp.sum(blk[:, None] * R >= pend[None, :], axis=1), N_EXPERTS - 1).astype(jnp.int32)
    live = blk < n_valid
    blk_e = jnp.where(live, blk_raw, blk_raw[jnp.maximum(n_valid - 1, 0)])
    blk_rows = jnp.where(live, jnp.clip(counts[blk_raw] - (blk * R - pstart[blk_raw]), 0, R), 0).astype(jnp.int32)
    xs = jnp.take(h, buf_tok, axis=0, mode="clip")
    y = moe_experts(xs, blk_e, blk_rows, n_valid.reshape(1), buf_gate, w_ein, w_eout, layer)
    dest2 = dest.reshape(T, TOP_K)
    return add3(x, jnp.take(y, dest2[:, 0], axis=0, mode="clip"), jnp.take(y, dest2[:, 1], axis=0, mode="clip"))


def _layer(x, groups, log_lb, log1m_lb, p, layer):
    dc = p['w_dw'].shape[1]
    dk = HGRN_HEADS * HGRN_DK
    dm = p['w_mem_out'].shape[1]
    D = x.shape[1]
    h = rmsnorm(x, p['norm_mix'], BF16)
    proj = dense([h], [p['w_in']], layer, F32, name="in_proj", tn=1024, tm_max=320)
    col_hgrn = 2 * dc
    col_mem = col_hgrn + 4 * dk
    col_gate = col_mem + dm
    cs, os_, oms, conv_states, hgrn_states = [], [], [], [], []
    for row0, B, L, conv_state, s0, mem_k, mem_v in groups:
        c, ns = conv_branch(proj, row0, B, L, conv_state, p['w_dw'], p['b_dw'], p['conv_ln_g'], p['conv_ln_b'])
        o, s_new = hgrn_branch(proj, row0, B, L, s0, log_lb, log1m_lb, p['hgrn_norm'], col_hgrn)
        om = attn_branch(proj, row0, B, L, mem_k.reshape(B, -1, dm), mem_v.reshape(B, -1, dm), col_mem)
        cs.append(c); os_.append(o); oms.append(om); conv_states.append(ns); hgrn_states.append(s_new)
    cat = lambda xs: xs[0] if len(xs) == 1 else jnp.concatenate(xs, axis=0)
    merged = dense([cat(cs), cat(os_), cat(oms)], [p['w_conv_out'], p['w_hgrn_out'], p['w_mem_out']], layer, BF16,
                   gate_src=proj, gate_cols=[col_gate, col_gate + D, col_gate + 2 * D], name="branch_merge")
    x = dense([merged], [p['w_out']], layer, F32, residual=x, name="out_proj", tn=1024, tm_max=320)
    x = hier_moe(x, p['norm_ffn'], p['w_router_group'], p['b_router_group'], p['w_router_expert'],
                 p['b_router_expert'], p['w_exp_in'], p['w_exp_out'], layer)
    return x, conv_states, hgrn_states


def kernel(x_prompt, x_sample, state_conv, state_hgrn, cache_mem_k, cache_mem_v, mem_prompt, norm_mix, norm_mem, norm_ffn, norm_final, w_in, w_dw, b_dw, conv_ln_g, conv_ln_b, w_conv_out, hgrn_lb, hgrn_norm, w_hgrn_out, w_mem_kv, w_mem_out, w_out, w_router_group, b_router_group, w_router_expert, b_router_expert, w_exp_in, w_exp_out):
    depth = w_in.shape[0]
    Bp, Lp, D = x_prompt.shape
    Bs, Ls, _ = x_sample.shape
    n_mem = mem_prompt.shape[1]
    dc = w_dw.shape[2]
    dm = w_mem_out.shape[1]
    lb_all = jnp.cumsum(jax.nn.softmax(hgrn_lb.astype(F32), axis=0), axis=0)
    lb_all = lb_all - lb_all[:1]
    log_lb = jnp.log(lb_all)
    log1m_lb = jnp.log1p(-lb_all)
    x = jnp.concatenate([x_prompt.reshape(Bp * Lp, D), x_sample.reshape(Bs * Ls, D)], axis=0)
    conv0 = jnp.zeros((Bp, CONV_WIDTH - 1, dc), F32)
    s0 = jnp.zeros((Bp, HGRN_HEADS, HGRN_DK, HGRN_DV), F32)
    mem_flat = mem_prompt.reshape(Bp * n_mem, D)
    conv_p, hgrn_p, mk_p, mv_p, conv_s, hgrn_s = [], [], [], [], [], []
    for l in range(depth):
        p = dict(norm_mix=norm_mix[l], w_in=w_in, w_dw=w_dw[l], b_dw=b_dw[l], conv_ln_g=conv_ln_g[l],
                 conv_ln_b=conv_ln_b[l], w_conv_out=w_conv_out, hgrn_norm=hgrn_norm[l], w_hgrn_out=w_hgrn_out,
                 w_mem_out=w_mem_out, w_out=w_out, norm_ffn=norm_ffn[l], w_router_group=w_router_group[l],
                 b_router_group=b_router_group[l], w_router_expert=w_router_expert[l],
                 b_router_expert=b_router_expert[l], w_exp_in=w_exp_in, w_exp_out=w_exp_out)
        kv = dense([rmsnorm(mem_flat, norm_mem[l], BF16)], [w_mem_kv], l, F32, name="mem_kv")
        mk = kv[:, :dm].reshape(Bp, n_mem, MEM_HEADS, dm // MEM_HEADS)
        mv = kv[:, dm:].reshape(Bp, n_mem, MEM_HEADS, dm // MEM_HEADS)
        groups = [(0, Bp, Lp, conv0, s0, mk, mv),
                  (Bp * Lp, Bs, Ls, state_conv[l], state_hgrn[l], cache_mem_k[l], cache_mem_v[l])]
        x, cstates, hstates = _layer(x, groups, log_lb[l], log1m_lb[l], p, l)
        conv_p.append(cstates[0]); hgrn_p.append(hstates[0]); mk_p.append(mk); mv_p.append(mv)
        conv_s.append(cstates[1]); hgrn_s.append(hstates[1])
    y = rmsnorm(x, norm_final, F32)
    y_prompt = y[:Bp * Lp].reshape(Bp, Lp, D)
    y_sample = y[Bp * Lp:].reshape(Bs, Ls, D)
    return (y_prompt, y_sample, jnp.stack(conv_p), jnp.stack(hgrn_p), jnp.stack(mk_p), jnp.stack(mv_p),
            jnp.stack(conv_s), jnp.stack(hgrn_s))
```

```python
import functools

import numpy as np
import jax
import jax.numpy as jnp
from jax import lax
from jax.experimental import pallas as pl
from jax.experimental.pallas import tpu as pltpu

F32 = jnp.float32
BF16 = jnp.bfloat16

EPS = 1e-6
CONV_WIDTH = 31
HGRN_HEADS = 16
HGRN_DK = 128
HGRN_DV = 128
MEM_HEADS = 4
N_GROUPS = 4
EXPERTS_PER_GROUP = 8
N_EXPERTS = N_GROUPS * EXPERTS_PER_GROUP
TOP_K = 2

LANES = 128
SUBLANES = 8
VMEM_LIMIT_BYTES = 56 * 1024 * 1024
MM_TN = 512
MOE_ROWS = 640
MOE_NC = 1024
MOE_SUB = 128
CONV_PAD = 32
HGRN_CHUNK = 128


def _cparams(*sem):
    return pltpu.CompilerParams(dimension_semantics=sem, vmem_limit_bytes=VMEM_LIMIT_BYTES)


def _pick_tile(n, cands):
    for c in cands:
        if n % c == 0:
            return c
    return n


def _sigmoid(x):
    return 1.0 / (1.0 + jnp.exp(-x))


def _dot(a, b):
    return jnp.dot(a, b, preferred_element_type=F32)


def _dot_nt(a, b):
    return lax.dot_general(a, b, (((1,), (1,)), ((), ())), preferred_element_type=F32)


def _dot_tn(a, b):
    return lax.dot_general(a, b, (((0,), (0,)), ((), ())), preferred_element_type=F32)


def _rms_body(x, g):
    ms = jnp.mean(x * x, axis=-1, keepdims=True)
    return x * lax.rsqrt(ms + EPS) * g


def _rms_kernel(x_ref, g_ref, o_ref):
    o_ref[...] = _rms_body(x_ref[...], g_ref[...]).astype(o_ref.dtype)


def rmsnorm(x, g, out_dtype):
    T, D = x.shape
    tm = _pick_tile(T, (320, 256, 128, 64, 32, 16, 8))
    return pl.pallas_call(
        _rms_kernel,
        out_shape=jax.ShapeDtypeStruct((T, D), out_dtype),
        grid=(T // tm,),
        in_specs=[pl.BlockSpec((tm, D), lambda i: (i, 0)), pl.BlockSpec((1, D), lambda i: (0, 0))],
        out_specs=pl.BlockSpec((tm, D), lambda i: (i, 0)),
        compiler_params=_cparams("parallel"),
        name="rmsnorm",
    )(x, g.reshape(1, D))


def _rms_split_kernel(x_ref, g_ref, o1_ref, o2_ref, *, n1_blocks):
    i = pl.program_id(0)
    y = _rms_body(x_ref[...], g_ref[...])

    @pl.when(i < n1_blocks)
    def _():
        o1_ref[...] = y

    @pl.when(i >= n1_blocks)
    def _():
        o2_ref[...] = y


def rmsnorm_split(x, g, n_first):
    T, D = x.shape
    tm = _pick_tile(int(np.gcd(n_first, T - n_first)), (256, 128, 64, 32, 16, 8))
    nb1 = n_first // tm
    return pl.pallas_call(
        functools.partial(_rms_split_kernel, n1_blocks=nb1),
        out_shape=(jax.ShapeDtypeStruct((n_first, D), F32), jax.ShapeDtypeStruct((T - n_first, D), F32)),
        grid=(T // tm,),
        in_specs=[pl.BlockSpec((tm, D), lambda i: (i, 0)), pl.BlockSpec((1, D), lambda i: (0, 0))],
        out_specs=(pl.BlockSpec((tm, D), lambda i: (jnp.minimum(i, nb1 - 1), 0)),
                   pl.BlockSpec((tm, D), lambda i: (jnp.maximum(i - nb1, 0), 0))),
        compiler_params=_cparams("arbitrary"),
        name="rmsnorm_final",
    )(x, g.reshape(1, D))


def _pack_pairs(h):
    half = h.shape[1] // 2
    bits = lax.bitcast_convert_type(h.astype(BF16).astype(F32), jnp.uint32)
    return bits[:, :half] | (bits[:, half:] >> 16)


def _unpack_pairs(w):
    hi = lax.bitcast_convert_type(w & jnp.uint32(0xFFFF0000), F32).astype(BF16)
    lo = lax.bitcast_convert_type(w << 16, F32).astype(BF16)
    return jnp.concatenate([hi, lo], axis=1)


def _unpack_pairs_f32(w):
    hi = lax.bitcast_convert_type(w & jnp.uint32(0xFFFF0000), F32)
    lo = lax.bitcast_convert_type(w << 16, F32)
    return hi, lo


def _rms_router_kernel(x_ref, g_ref, wr_ref, h_ref, lg_ref):
    h = _rms_body(x_ref[...], g_ref[...])
    h_ref[...] = _pack_pairs(h)
    lg_ref[...] = _dot(h.astype(BF16), wr_ref[...].astype(BF16))


def rmsnorm_router(x, g, w_router):
    T, D = x.shape
    R = w_router.shape[1]
    tm = _pick_tile(T, (320, 256, 128, 64, 32, 16, 8))
    return pl.pallas_call(
        _rms_router_kernel,
        out_shape=(jax.ShapeDtypeStruct((T, D // 2), jnp.uint32), jax.ShapeDtypeStruct((T, R), F32)),
        grid=(T // tm,),
        in_specs=[pl.BlockSpec((tm, D), lambda i: (i, 0)), pl.BlockSpec((1, D), lambda i: (0, 0)),
                  pl.BlockSpec((D, R), lambda i: (0, 0))],
        out_specs=(pl.BlockSpec((tm, D // 2), lambda i: (i, 0)), pl.BlockSpec((tm, R), lambda i: (i, 0))),
        compiler_params=_cparams("parallel"),
        name="rmsnorm_router",
    )(x, g.reshape(1, D), w_router)


def _combine_kernel(x_ref, y_ref, g_ref, o_ref, *, nc):
    half = y_ref.shape[1] // 2
    g = g_ref[...]
    for n in range(2 * half // nc):
        cw = slice(n * (nc // 2), (n + 1) * (nc // 2))
        hi0, lo0 = _unpack_pairs_f32(y_ref[:, cw])
        hi1, lo1 = _unpack_pairs_f32(y_ref[:, half + n * (nc // 2):half + (n + 1) * (nc // 2)])
        c0 = n * nc
        o_ref[:, c0:c0 + nc // 2] = x_ref[:, c0:c0 + nc // 2] + (hi0 * g[:, 0:1] + hi1 * g[:, 1:2])
        o_ref[:, c0 + nc // 2:c0 + nc] = x_ref[:, c0 + nc // 2:c0 + nc] + (lo0 * g[:, 0:1] + lo1 * g[:, 1:2])


def moe_combine(x, y2, gate, nc):
    T, D = x.shape
    tm = _pick_tile(T, (320, 256, 128, 64, 32, 16, 8))
    spec = pl.BlockSpec((tm, D), lambda i: (i, 0))
    return pl.pallas_call(
        functools.partial(_combine_kernel, nc=nc), out_shape=jax.ShapeDtypeStruct((T, D), F32), grid=(T // tm,),
        in_specs=[spec, spec, pl.BlockSpec((tm, TOP_K), lambda i: (i, 0))], out_specs=spec,
        compiler_params=_cparams("parallel"), name="moe_combine",
    )(x, y2, gate)


def _mm_kernel(*refs, n_terms, gated, residual):
    a_refs = refs[:n_terms]
    w_refs = refs[n_terms:2 * n_terms]
    pos = 2 * n_terms
    g_refs = refs[pos:pos + n_terms] if gated else ()
    pos += n_terms if gated else 0
    r_ref = refs[pos] if residual else None
    pos += 1 if residual else 0
    o_ref = refs[pos]
    wb_refs = refs[pos + 1:pos + 1 + n_terms]

    @pl.when(pl.program_id(1) == 0)
    def _():
        for w_ref, wb_ref in zip(w_refs, wb_refs):
            wb_ref[...] = w_ref[...].astype(BF16)

    acc = None
    for i in range(n_terms):
        y = _dot(a_refs[i][...], wb_refs[i][...])
        if gated:
            y = y * _sigmoid(g_refs[i][...])
        acc = y if acc is None else acc + y
    if residual:
        acc = acc + r_ref[...]
    o_ref[...] = acc.astype(o_ref.dtype)


def dense(a_list, w_list, layer, out_dtype, gate_src=None, gate_cols=None, residual=None, name="dense",
          tn=MM_TN, tm_max=640):
    n_terms = len(a_list)
    T = a_list[0].shape[0]
    N = w_list[0].shape[2]
    tm = _pick_tile(T, tuple(c for c in (640, 512, 320, 256, 128, 64, 32, 16, 8) if c <= tm_max))
    in_specs, args = [], []
    for a in a_list:
        in_specs.append(pl.BlockSpec((tm, a.shape[1]), lambda n, m: (m, 0)))
        args.append(a)
    for w in w_list:
        in_specs.append(pl.BlockSpec((None, w.shape[1], tn), lambda n, m: (layer, 0, n)))
        args.append(w)
    if gate_src is not None:
        for c0 in gate_cols:
            in_specs.append(pl.BlockSpec((tm, tn), functools.partial(lambda n, m, cb: (m, cb + n), cb=c0 // tn)))
            args.append(gate_src)
    if residual is not None:
        in_specs.append(pl.BlockSpec((tm, tn), lambda n, m: (m, n)))
        args.append(residual)
    kern = functools.partial(_mm_kernel, n_terms=n_terms, gated=gate_src is not None,
                             residual=residual is not None)
    return pl.pallas_call(
        kern,
        out_shape=jax.ShapeDtypeStruct((T, N), out_dtype),
        grid=(N // tn, T // tm),
        in_specs=in_specs,
        out_specs=pl.BlockSpec((tm, tn), lambda n, m: (m, n)),
        scratch_shapes=[pltpu.VMEM((w.shape[1], tn), BF16) for w in w_list],
        compiler_params=_cparams("arbitrary", "arbitrary"),
        name=name,
    )(*args)


def _conv_kernel(a_ref, b_ref, st_ref, w_ref, bdw_ref, lng_ref, lnb_ref, c_ref, ns_ref, ubuf, sh, cbuf, *, tl, rc):
    W1 = CONV_WIDTH - 1

    @pl.when(pl.program_id(1) == 0)
    def _():
        ubuf[0:W1, :] = st_ref[...]

    ubuf[W1:W1 + tl, :] = a_ref[...] * _sigmoid(b_ref[...])
    for r in range(1, SUBLANES):
        span = tl + SUBLANES * ((W1 - r) // SUBLANES)
        sh[r - 1, 0:span, :] = ubuf[r:r + span, :]
    dc = ubuf.shape[1]
    lc = LANES

    def chunk(i, carry):
        r0 = pl.multiple_of(i * rc, SUBLANES)
        for c0 in range(0, dc, lc):
            acc = jnp.zeros((rc, lc), F32)
            for j in range(CONV_WIDTH):
                q, r = divmod(j, SUBLANES)
                src = ubuf if r == 0 else sh.at[r - 1]
                acc = acc + src[pl.ds(r0 + SUBLANES * q, rc), c0:c0 + lc] * w_ref[j:j + 1, c0:c0 + lc]
            cbuf[pl.ds(r0, rc), c0:c0 + lc] = acc
        y = cbuf[pl.ds(r0, rc), :] + bdw_ref[...]
        mu = jnp.mean(y, axis=-1, keepdims=True)
        d = y - mu
        var = jnp.mean(d * d, axis=-1, keepdims=True)
        z = d * lax.rsqrt(var + EPS) * lng_ref[...] + lnb_ref[...]
        c_ref[pl.ds(r0, rc), :] = (z * _sigmoid(z)).astype(c_ref.dtype)
        return carry

    lax.fori_loop(0, tl // rc, chunk, 0)
    tail = ubuf[tl:tl + W1, :]
    ns_ref[...] = tail
    ubuf[0:W1, :] = tail


def conv_branch(proj, row0, B, L, state, w_dw, b_dw, ln_g, ln_b):
    dc = w_dw.shape[1]
    tl = _pick_tile(L, (512, 256, 128, 64, 32, 16))
    rc = min(tl, 64)
    nl = L // tl
    rb0 = row0 // tl
    kern = functools.partial(_conv_kernel, tl=tl, rc=rc)
    vec = pl.BlockSpec((1, dc), lambda b, l: (0, 0))
    return pl.pallas_call(
        kern,
        out_shape=(jax.ShapeDtypeStruct((B * L, dc), BF16),
                   jax.ShapeDtypeStruct((B, CONV_WIDTH - 1, dc), F32)),
        grid=(B, nl),
        in_specs=[pl.BlockSpec((tl, dc), lambda b, l: (rb0 + b * nl + l, 0)),
                  pl.BlockSpec((tl, dc), lambda b, l: (rb0 + b * nl + l, 1)),
                  pl.BlockSpec((None, CONV_WIDTH - 1, dc), lambda b, l: (b, 0, 0)),
                  pl.BlockSpec((CONV_WIDTH, dc), lambda b, l: (0, 0)), vec, vec, vec],
        out_specs=(pl.BlockSpec((tl, dc), lambda b, l: (b * nl + l, 0)),
                   pl.BlockSpec((None, CONV_WIDTH - 1, dc), lambda b, l: (b, 0, 0))),
        scratch_shapes=[pltpu.VMEM((tl + CONV_PAD, dc), F32),
                        pltpu.VMEM((SUBLANES - 1, tl + CONV_PAD - SUBLANES, dc), F32),
                        pltpu.VMEM((tl, dc), F32)],
        compiler_params=_cparams("arbitrary", "arbitrary"),
        name="conv_branch",
    )(proj, proj, state, w_dw, b_dw.reshape(1, dc), ln_g.reshape(1, dc), ln_b.reshape(1, dc))


def _pair_levels(C):
    t = np.arange(C)[:, None]
    s = np.arange(C)[None, :]
    x = np.bitwise_xor(t, s)
    lvl = np.where(x > 0, np.floor(np.log2(np.maximum(x, 1))).astype(np.int32) + 1, 0)
    return np.where(s <= t, lvl, -1).astype(np.int32)


def _group_boundary(b_ref, hs, m, C):
    if 2 * m >= SUBLANES:
        pieces = [jnp.broadcast_to(b_ref[pl.ds(g * 2 * m + m - 1, 1), hs], (2 * m, LANES))
                  for g in range(C // (2 * m))]
        return pieces[0] if len(pieces) == 1 else jnp.concatenate(pieces, axis=0)
    sub = lax.broadcasted_iota(jnp.int32, (SUBLANES, LANES), 0)
    pieces = []
    for v in range(C // SUBLANES):
        piece = jnp.broadcast_to(b_ref[pl.ds(v * SUBLANES + m - 1, 1), hs], (SUBLANES, LANES))
        for j in range(1, SUBLANES // (2 * m)):
            nxt = jnp.broadcast_to(b_ref[pl.ds(v * SUBLANES + j * 2 * m + m - 1, 1), hs], (SUBLANES, LANES))
            piece = jnp.where(sub >= j * 2 * m, nxt, piece)
        pieces.append(piece)
    return jnp.concatenate(pieces, axis=0)


def _hgrn_kernel(f_ref, i_ref, q_ref, g_ref, s0_ref, loglb_ref, log1m_ref, nw_ref, tri_ref, lvl_ref,
                 o_ref, sout_ref, st_ref, lf_ref, b_ref, *, C):
    cidx = pl.program_id(1)

    @pl.when(cidx == 0)
    def _():
        def init(h, carry):
            st_ref[h] = s0_ref[h].T
            return carry
        lax.fori_loop(0, HGRN_HEADS, init, 0)

    x = f_ref[...]
    log_sig = jnp.minimum(x, 0.0) - jnp.log(1.0 + jnp.exp(-jnp.abs(x)))
    la = loglb_ref[...]
    lc = log1m_ref[...] + log_sig
    lf = jnp.maximum(la, lc) + jnp.log(1.0 + jnp.exp(-jnp.abs(la - lc)))
    lf_ref[...] = lf
    hi = lf.astype(BF16)
    lo = (lf - hi.astype(F32)).astype(BF16)
    b_ref[...] = _dot(tri_ref[...], hi) + _dot(tri_ref[...], lo)

    row = lax.broadcasted_iota(jnp.int32, (C, LANES), 0)

    def head(h, carry):
        hs = pl.ds(pl.multiple_of(h * LANES, LANES), LANES)
        lf_h = lf_ref[:, hs]
        b = b_ref[:, hs]
        q = q_ref[:, hs]
        v = i_ref[:, hs].astype(BF16)
        k = 1.0 - jnp.exp(lf_h)
        lvl = lvl_ref[...]
        p = jnp.where(lvl == 0, _dot_nt(q.astype(BF16), k.astype(BF16)), 0.0)
        m, level = 1, 1
        while m < C:
            upper = (row & m) != 0
            if m == 1:
                xpo = jnp.where(upper, lf_h, 0.0)
            else:
                bnd = _group_boundary(b_ref, hs, m, C)
                xpo = jnp.where(upper, b - bnd, bnd - b)
            ex = jnp.exp(xpo)
            pm = _dot_nt((q * ex).astype(BF16), (k * ex).astype(BF16))
            p = jnp.where(lvl == level, pm, p)
            m *= 2
            level += 1
        s_t = st_ref[h]
        o = _dot(p.astype(BF16), v) + _dot_nt((q * jnp.exp(b)).astype(BF16), s_t.astype(BF16))
        b_last = b_ref[pl.ds(C - 1, 1), hs]
        k_dec = (k * jnp.exp(b_last - b)).astype(BF16)
        st_ref[h] = s_t * jnp.exp(b_last) + _dot_tn(v, k_dec)
        ms = jnp.mean(o * o, axis=-1, keepdims=True)
        gv = g_ref[:, hs]
        o_ref[:, hs] = (o * lax.rsqrt(ms + EPS) * nw_ref[:, hs] * (gv * _sigmoid(gv))).astype(o_ref.dtype)
        return carry

    lax.fori_loop(0, HGRN_HEADS, head, 0, unroll=2)

    @pl.when(cidx == pl.num_programs(1) - 1)
    def _():
        def fin(h, carry):
            sout_ref[h] = st_ref[h].T
            return carry
        lax.fori_loop(0, HGRN_HEADS, fin, 0)


def hgrn_branch(proj, row0, B, L, s0, log_lb, log1m_lb, norm_w, col0):
    dk = HGRN_HEADS * HGRN_DK
    C = _pick_tile(L, (HGRN_CHUNK, 64, 32, 16))
    nc = L // C
    rb0 = row0 // C
    cb0 = col0 // dk
    tri = jnp.asarray(np.tril(np.ones((C, C), np.float32)), BF16)
    lvl = jnp.asarray(_pair_levels(C))
    kern = functools.partial(_hgrn_kernel, C=C)

    def pspec(j):
        return pl.BlockSpec((C, dk), functools.partial(lambda b, c, j: (rb0 + b * nc + c, cb0 + j), j=j))

    vec = pl.BlockSpec((1, dk), lambda b, c: (0, 0))
    sspec = pl.BlockSpec((None, HGRN_HEADS, HGRN_DK, HGRN_DV), lambda b, c: (b, 0, 0, 0))
    return pl.pallas_call(
        kern,
        out_shape=(jax.ShapeDtypeStruct((B * L, dk), BF16),
                   jax.ShapeDtypeStruct((B, HGRN_HEADS, HGRN_DK, HGRN_DV), F32)),
        grid=(B, nc),
        in_specs=[pspec(0), pspec(1), pspec(2), pspec(3), sspec, vec, vec, vec,
                  pl.BlockSpec((C, C), lambda b, c: (0, 0)), pl.BlockSpec((C, C), lambda b, c: (0, 0))],
        out_specs=(pl.BlockSpec((C, dk), lambda b, c: (b * nc + c, 0)), sspec),
        scratch_shapes=[pltpu.VMEM((HGRN_HEADS, HGRN_DV, HGRN_DK), F32), pltpu.VMEM((C, dk), F32),
                        pltpu.VMEM((C, dk), F32)],
        compiler_params=_cparams("arbitrary", "arbitrary"),
        name="hgrn_branch",
    )(proj, proj, proj, proj, s0, log_lb.reshape(1, dk), log1m_lb.reshape(1, dk), norm_w.reshape(1, dk), tri, lvl)


def _attn_kernel(q_ref, k_ref, v_ref, o_ref, *, scale):
    hd = q_ref.shape[1] // MEM_HEADS
    for h in range(MEM_HEADS):
        cs = slice(h * hd, (h + 1) * hd)
        s = _dot_nt(q_ref[:, cs].astype(BF16), k_ref[:, cs].astype(BF16)) * scale
        s = s - jnp.max(s, axis=-1, keepdims=True)
        e = jnp.exp(s)
        pr = e / jnp.sum(e, axis=-1, keepdims=True)
        o_ref[:, cs] = _dot(pr.astype(BF16), v_ref[:, cs].astype(BF16)).astype(o_ref.dtype)


def attn_branch(proj, row0, B, L, mem_k, mem_v, col0):
    n_mem, dm = mem_k.shape[1], mem_k.shape[2]
    tl = _pick_tile(L, (512, 256, 128, 64, 32, 16))
    nl = L // tl
    rb0 = row0 // tl
    cb0 = col0 // dm
    kern = functools.partial(_attn_kernel, scale=float((dm // MEM_HEADS) ** -0.5))
    mspec = pl.BlockSpec((None, n_mem, dm), lambda b, l: (b, 0, 0))
    return pl.pallas_call(
        kern,
        out_shape=jax.ShapeDtypeStruct((B * L, dm), BF16),
        grid=(B, nl),
        in_specs=[pl.BlockSpec((tl, dm), lambda b, l: (rb0 + b * nl + l, cb0)), mspec, mspec],
        out_specs=pl.BlockSpec((tl, dm), lambda b, l: (b * nl + l, 0)),
        compiler_params=_cparams("parallel", "parallel"),
        name="attn_branch",
    )(proj, mem_k, mem_v)


def _moe_kernel(be_ref, nr_ref, nv_ref, x_ref, win_ref, wout_ref, o_ref, acc_ref, hm_ref, wb_ref, *, n1, sb):
    b = pl.program_id(0)
    s = pl.program_id(1)
    nr = nr_ref[b]
    R = x_ref.shape[0]
    F, nc = wout_ref.shape
    nsub = (nr + sb - 1) // sb

    @pl.when(s < n1)
    def _():
        wb_ref[...] = win_ref[...].astype(BF16)
        shift = ((s % 2) * 16).astype(jnp.uint32)
        for k in range(1, R // sb + 1):
            rows = slice(0, k * sb)

            @pl.when(nsub == k)
            def _():
                xk = lax.bitcast_convert_type((x_ref[rows, :] << shift) & jnp.uint32(0xFFFF0000), F32).astype(BF16)
                part = _dot(xk, wb_ref[...])

                @pl.when(s == 0)
                def _():
                    acc_ref[rows, :] = part

                @pl.when(s > 0)
                def _():
                    acc_ref[rows, :] += part

                @pl.when(s == n1 - 1)
                def _():
                    a = acc_ref[rows, :F]
                    hm_ref[rows, :] = (a * _sigmoid(a) * acc_ref[rows, F:]).astype(BF16)

    @pl.when(s >= n1)
    def _():
        wb_ref[:, :nc] = wout_ref[...].astype(BF16)
        for k in range(0, R // sb + 1):
            @pl.when(nsub == k)
            def _():
                if k > 0:
                    rows = slice(0, k * sb)
                    o_ref[rows, :] = _pack_pairs(_dot(hm_ref[rows, :], wb_ref[:, :nc]))
                if k < R // sb:
                    o_ref[k * sb:, :] = jnp.zeros((R - k * sb, nc // 2), jnp.uint32)


def moe_experts(xs, blk_e, blk_rows, n_valid, w_ein, w_eout, layer):
    P = xs.shape[0]
    D, F2 = w_ein.shape[2], w_ein.shape[3]
    F = w_eout.shape[2]
    R = MOE_ROWS
    nb = P // R
    n1 = 4
    kc = D // n1
    nc = MOE_NC
    n2 = D // nc

    def x_chunk(b, s, be, nr, nv):
        return (jnp.minimum(b, nv[0] - 1), jnp.where(b < nv[0], jnp.minimum(s // 2, n1 // 2 - 1), n1 // 2 - 1))

    def in_chunk(b, s, be, nr, nv):
        sc = jnp.minimum(s, n1 - 1)
        return (layer, be[b], jnp.where(b < nv[0], (sc % 2) * (n1 // 2) + sc // 2, n1 - 1), 0)

    def out_chunk(b, s, be, nr, nv):
        early = s < 2
        e = jnp.where(early, be[jnp.maximum(b - 1, 0)], be[b])
        col = jnp.where(jnp.logical_or(early, b >= nv[0]), n2 - 1, jnp.maximum(s - n1, 0))
        return (layer, e, 0, col)

    def y_chunk(b, s, be, nr, nv):
        spare = b > nv[0]
        return (jnp.where(spare, nv[0], b), jnp.where(spare, n2 - 1, jnp.maximum(s - n1, 0)))

    return pl.pallas_call(
        functools.partial(_moe_kernel, n1=n1, sb=MOE_SUB),
        out_shape=jax.ShapeDtypeStruct((P, D // 2), jnp.uint32),
        grid_spec=pltpu.PrefetchScalarGridSpec(
            num_scalar_prefetch=3, grid=(nb, n1 + n2),
            in_specs=[pl.BlockSpec((R, kc), x_chunk),
                      pl.BlockSpec((None, None, kc, F2), in_chunk),
                      pl.BlockSpec((None, None, F, nc), out_chunk)],
            out_specs=pl.BlockSpec((R, nc // 2), y_chunk),
            scratch_shapes=[pltpu.VMEM((R, F2), F32), pltpu.VMEM((R, F), BF16), pltpu.VMEM((kc, F2), BF16)]),
        compiler_params=_cparams("arbitrary", "arbitrary"),
        name="moe_experts",
    )(blk_e, blk_rows, n_valid, xs, w_ein, w_eout)


def _route(logits, b_rg, b_re):
    T = logits.shape[0]
    g_logits = logits[:, :N_GROUPS] + b_rg
    g_prob = jax.nn.softmax(g_logits, axis=-1)
    grp = jnp.argmax(g_logits, axis=-1).astype(jnp.int32)
    p_grp = jnp.take_along_axis(g_prob, grp[:, None], axis=-1)
    e_logits = (logits[:, N_GROUPS:N_GROUPS + N_EXPERTS] + b_re).reshape(T, N_GROUPS, EXPERTS_PER_GROUP)
    e_sel = jnp.take_along_axis(e_logits, grp[:, None, None], axis=1)[:, 0]
    top_val, top_idx = lax.top_k(e_sel, TOP_K)
    gate = p_grp * jax.nn.softmax(top_val, axis=-1)
    eid = grp[:, None] * EXPERTS_PER_GROUP + top_idx
    return eid, gate


def hier_moe(x, norm_w, w_rg, b_rg, w_re, b_re, w_ein, w_eout, layer):
    T, D = x.shape
    w_router = jnp.concatenate([w_rg, w_re, jnp.zeros((D, LANES - N_GROUPS - N_EXPERTS), F32)], axis=1)
    h, logits = rmsnorm_router(x, norm_w, w_router)
    eid, gate = _route(logits, b_rg, b_re)
    R = MOE_ROWS
    A = T * TOP_K
    flat_e = eid.reshape(-1)
    onehot = (flat_e[:, None] == jnp.arange(N_EXPERTS, dtype=jnp.int32)[None, :]).astype(jnp.int32)
    csum = jnp.cumsum(onehot, axis=0)
    rank = jnp.take_along_axis(csum, flat_e[:, None], axis=1)[:, 0] - 1
    counts = csum[-1]
    padded = (counts + R - 1) // R * R
    pend = jnp.cumsum(padded)
    pstart = pend - padded
    dest = pstart[flat_e] + rank
    nb = -(-(A + N_EXPERTS * (R - 1)) // R)
    P = nb * R
    flat_tok = jnp.repeat(jnp.arange(T, dtype=jnp.int32), TOP_K)
    buf_tok = (jnp.arange(P, dtype=jnp.int32) % T).at[dest].set(flat_tok)
    blk = jnp.arange(nb, dtype=jnp.int32)
    n_valid = (pend[-1] // R).astype(jnp.int32)
    blk_raw = jnp.minimum(jnp.sum(blk[:, None] * R >= pend[None, :], axis=1), N_EXPERTS - 1).astype(jnp.int32)
    live = blk < n_valid
    blk_e = jnp.where(live, blk_raw, blk_raw[jnp.maximum(n_valid - 1, 0)])
    blk_rows = jnp.where(live, jnp.clip(counts[blk_raw] - (blk * R - pstart[blk_raw]), 0, R), 0).astype(jnp.int32)
    xs = jnp.take(h, buf_tok, axis=0, mode="clip")
    y = moe_experts(xs, blk_e, blk_rows, n_valid.reshape(1), w_ein, w_eout, layer)
    y2 = jnp.take(y, dest, axis=0, mode="clip").reshape(T, D)
    return moe_combine(x, y2, gate, MOE_NC)


def _layer(x, groups, log_lb, log1m_lb, p, layer):
    dc = p['w_dw'].shape[1]
    dk = HGRN_HEADS * HGRN_DK
    dm = p['w_mem_out'].shape[1]
    D = x.shape[1]
    h = rmsnorm(x, p['norm_mix'], BF16)
    proj = dense([h], [p['w_in']], layer, F32, name="in_proj", tn=1024, tm_max=320)
    col_hgrn = 2 * dc
    col_mem = col_hgrn + 4 * dk
    col_gate = col_mem + dm
    cs, os_, oms, conv_states, hgrn_states = [], [], [], [], []
    for row0, B, L, conv_state, s0, mem_k, mem_v in groups:
        c, ns = conv_branch(proj, row0, B, L, conv_state, p['w_dw'], p['b_dw'], p['conv_ln_g'], p['conv_ln_b'])
        o, s_new = hgrn_branch(proj, row0, B, L, s0, log_lb, log1m_lb, p['hgrn_norm'], col_hgrn)
        om = attn_branch(proj, row0, B, L, mem_k.reshape(B, -1, dm), mem_v.reshape(B, -1, dm), col_mem)
        cs.append(c); os_.append(o); oms.append(om); conv_states.append(ns); hgrn_states.append(s_new)
    cat = lambda xs: xs[0] if len(xs) == 1 else jnp.concatenate(xs, axis=0)
    merged = dense([cat(cs), cat(os_), cat(oms)], [p['w_conv_out'], p['w_hgrn_out'], p['w_mem_out']], layer, BF16,
                   gate_src=proj, gate_cols=[col_gate, col_gate + D, col_gate + 2 * D], name="branch_merge")
    x = dense([merged], [p['w_out']], layer, F32, residual=x, name="out_proj", tn=1024, tm_max=320)
    x = hier_moe(x, p['norm_ffn'], p['w_router_group'], p['b_router_group'], p['w_router_expert'],
                 p['b_router_expert'], p['w_exp_in'], p['w_exp_out'], layer)
    return x, conv_states, hgrn_states


def kernel(x_prompt, x_sample, state_conv, state_hgrn, cache_mem_k, cache_mem_v, mem_prompt, norm_mix, norm_mem, norm_ffn, norm_final, w_in, w_dw, b_dw, conv_ln_g, conv_ln_b, w_conv_out, hgrn_lb, hgrn_norm, w_hgrn_out, w_mem_kv, w_mem_out, w_out, w_router_group, b_router_group, w_router_expert, b_router_expert, w_exp_in, w_exp_out):
    depth = w_in.shape[0]
    Bp, Lp, D = x_prompt.shape
    Bs, Ls, _ = x_sample.shape
    n_mem = mem_prompt.shape[1]
    dc = w_dw.shape[2]
    dm = w_mem_out.shape[1]
    lb_all = jnp.cumsum(jax.nn.softmax(hgrn_lb.astype(F32), axis=0), axis=0)
    lb_all = lb_all - lb_all[:1]
    log_lb = jnp.log(lb_all)
    log1m_lb = jnp.log1p(-lb_all)
    x = jnp.concatenate([x_prompt.reshape(Bp * Lp, D), x_sample.reshape(Bs * Ls, D)], axis=0)
    conv0 = jnp.zeros((Bp, CONV_WIDTH - 1, dc), F32)
    s0 = jnp.zeros((Bp, HGRN_HEADS, HGRN_DK, HGRN_DV), F32)
    mem_flat = mem_prompt.reshape(Bp * n_mem, D)
    conv_p, hgrn_p, mk_p, mv_p, conv_s, hgrn_s = [], [], [], [], [], []
    for l in range(depth):
        p = dict(norm_mix=norm_mix[l], w_in=w_in, w_dw=w_dw[l], b_dw=b_dw[l], conv_ln_g=conv_ln_g[l],
                 conv_ln_b=conv_ln_b[l], w_conv_out=w_conv_out, hgrn_norm=hgrn_norm[l], w_hgrn_out=w_hgrn_out,
                 w_mem_out=w_mem_out, w_out=w_out, norm_ffn=norm_ffn[l], w_router_group=w_router_group[l],
                 b_router_group=b_router_group[l], w_router_expert=w_router_expert[l],
                 b_router_expert=b_router_expert[l], w_exp_in=w_exp_in, w_exp_out=w_exp_out)
        kv = dense([rmsnorm(mem_flat, norm_mem[l], BF16)], [w_mem_kv], l, F32, name="mem_kv")
        mk = kv[:, :dm].reshape(Bp, n_mem, MEM_HEADS, dm // MEM_HEADS)
        mv = kv[:, dm:].reshape(Bp, n_mem, MEM_HEADS, dm // MEM_HEADS)
        groups = [(0, Bp, Lp, conv0, s0, mk, mv),
                  (Bp * Lp, Bs, Ls, state_conv[l], state_hgrn[l], cache_mem_k[l], cache_mem_v[l])]
        x, cstates, hstates = _layer(x, groups, log_lb[l], log1m_lb[l], p, l)
        conv_p.append(cstates[0]); hgrn_p.append(hstates[0]); mk_p.append(mk); mv_p.append(mv)
        conv_s.append(cstates[1]); hgrn_s.append(hstates[1])
    y_prompt, y_sample = rmsnorm_split(x, norm_final, Bp * Lp)
    y_prompt = y_prompt.reshape(Bp, Lp, D)
    y_sample = y_sample.reshape(Bs, Ls, D)
    return (y_prompt, y_sample, jnp.stack(conv_p), jnp.stack(hgrn_p), jnp.stack(mk_p), jnp.stack(mv_p),
            jnp.stack(conv_s), jnp.stack(hgrn_s))
```

```python
import functools

import numpy as np
import jax
import jax.numpy as jnp
from jax import lax
from jax.experimental import pallas as pl
from jax.experimental.pallas import tpu as pltpu

F32 = jnp.float32
BF16 = jnp.bfloat16

EPS = 1e-6
CONV_WIDTH = 31
HGRN_HEADS = 16
HGRN_DK = 128
HGRN_DV = 128
MEM_HEADS = 4
N_GROUPS = 4
EXPERTS_PER_GROUP = 8
N_EXPERTS = N_GROUPS * EXPERTS_PER_GROUP
TOP_K = 2

LANES = 128
SUBLANES = 8
VMEM_LIMIT_BYTES = 56 * 1024 * 1024
MM_TN = 512
MOE_ROWS = 640
MOE_NC = 1024
MOE_SUB = 128
CONV_PAD = 32
HGRN_CHUNK = 128
HGRN_PRECISE_MAX_LEN = 64


def _cparams(*sem):
    return pltpu.CompilerParams(dimension_semantics=sem, vmem_limit_bytes=VMEM_LIMIT_BYTES)


def _pick_tile(n, cands):
    for c in cands:
        if n % c == 0:
            return c
    return n


def _sigmoid(x):
    return 1.0 / (1.0 + jnp.exp(-x))


def _dot(a, b, precision=None):
    return jnp.dot(a, b, preferred_element_type=F32, precision=precision)


def _dot_nt(a, b, precision=None):
    return lax.dot_general(a, b, (((1,), (1,)), ((), ())), preferred_element_type=F32, precision=precision)


def _dot_tn(a, b, precision=None):
    return lax.dot_general(a, b, (((0,), (0,)), ((), ())), preferred_element_type=F32, precision=precision)


def _rms_body(x, g):
    ms = jnp.mean(x * x, axis=-1, keepdims=True)
    return x * lax.rsqrt(ms + EPS) * g


def _rms_kernel(x_ref, g_ref, o_ref):
    o_ref[...] = _rms_body(x_ref[...], g_ref[...]).astype(o_ref.dtype)


def rmsnorm(x, g, out_dtype):
    T, D = x.shape
    tm = _pick_tile(T, (320, 256, 128, 64, 32, 16, 8))
    return pl.pallas_call(
        _rms_kernel,
        out_shape=jax.ShapeDtypeStruct((T, D), out_dtype),
        grid=(T // tm,),
        in_specs=[pl.BlockSpec((tm, D), lambda i: (i, 0)), pl.BlockSpec((1, D), lambda i: (0, 0))],
        out_specs=pl.BlockSpec((tm, D), lambda i: (i, 0)),
        compiler_params=_cparams("parallel"),
        name="rmsnorm",
    )(x, g.reshape(1, D))


def _pack_pairs(h):
    half = h.shape[1] // 2
    bits = lax.bitcast_convert_type(h.astype(BF16).astype(F32), jnp.uint32)
    return bits[:, :half] | (bits[:, half:] >> 16)


def _unpack_pairs_f32(w):
    hi = lax.bitcast_convert_type(w & jnp.uint32(0xFFFF0000), F32)
    lo = lax.bitcast_convert_type(w << 16, F32)
    return hi, lo


def _rms_router_kernel(x_ref, g_ref, wr_ref, h_ref, lg_ref):
    h = _rms_body(x_ref[...], g_ref[...])
    h_ref[...] = _pack_pairs(h)
    lg_ref[...] = jnp.dot(h, wr_ref[...], precision=lax.Precision.HIGHEST, preferred_element_type=F32)


def rmsnorm_router(x, g, w_router):
    T, D = x.shape
    R = w_router.shape[1]
    tm = _pick_tile(T, (320, 256, 128, 64, 32, 16, 8))
    return pl.pallas_call(
        _rms_router_kernel,
        out_shape=(jax.ShapeDtypeStruct((T, D // 2), jnp.uint32), jax.ShapeDtypeStruct((T, R), F32)),
        grid=(T // tm,),
        in_specs=[pl.BlockSpec((tm, D), lambda i: (i, 0)), pl.BlockSpec((1, D), lambda i: (0, 0)),
                  pl.BlockSpec((D, R), lambda i: (0, 0))],
        out_specs=(pl.BlockSpec((tm, D // 2), lambda i: (i, 0)), pl.BlockSpec((tm, R), lambda i: (i, 0))),
        compiler_params=_cparams("parallel"),
        name="rmsnorm_router",
    )(x, g.reshape(1, D), w_router)


def _combine_kernel(x_ref, y0_ref, y1_ref, g_ref, nw_ref, o1_ref, o2_ref, xbuf, *, nc, n_first_blocks):
    D = x_ref.shape[1]
    g = g_ref[...]
    g0, g1 = g[:, 0:1], g[:, 1:2]
    ss = jnp.zeros((x_ref.shape[0], 1), F32)
    for n in range(D // nc):
        cw = slice(n * (nc // 2), (n + 1) * (nc // 2))
        hi0, lo0 = _unpack_pairs_f32(y0_ref[:, cw])
        hi1, lo1 = _unpack_pairs_f32(y1_ref[:, cw])
        ca = slice(n * nc, n * nc + nc // 2)
        cb = slice(n * nc + nc // 2, (n + 1) * nc)
        xa = x_ref[:, ca] + (hi0 * g0 + hi1 * g1)
        xb = x_ref[:, cb] + (lo0 * g0 + lo1 * g1)
        xbuf[:, ca] = xa
        xbuf[:, cb] = xb
        ss = ss + jnp.sum(xa * xa, axis=-1, keepdims=True) + jnp.sum(xb * xb, axis=-1, keepdims=True)
    xn = xbuf[...]
    y = xn * lax.rsqrt(ss * (1.0 / D) + EPS) * nw_ref[...]
    if n_first_blocks is None:
        o1_ref[...] = xn
        o2_ref[...] = y.astype(o2_ref.dtype)
    else:
        i = pl.program_id(0)

        @pl.when(i < n_first_blocks)
        def _():
            o1_ref[...] = y

        @pl.when(i >= n_first_blocks)
        def _():
            o2_ref[...] = y


def moe_combine(x, y0, y1, gate, norm_w, nc, n_first=None):
    T, D = x.shape
    tm = _pick_tile(T if n_first is None else int(np.gcd(n_first, T - n_first)), (128, 64, 32, 16, 8))
    spec = pl.BlockSpec((tm, D), lambda i: (i, 0))
    yspec = pl.BlockSpec((tm, D // 2), lambda i: (i, 0))
    if n_first is None:
        nb1 = None
        out_shape = (jax.ShapeDtypeStruct((T, D), F32), jax.ShapeDtypeStruct((T, D), BF16))
        out_specs = (spec, spec)
    else:
        nb1 = n_first // tm
        out_shape = (jax.ShapeDtypeStruct((n_first, D), F32), jax.ShapeDtypeStruct((T - n_first, D), F32))
        out_specs = (pl.BlockSpec((tm, D), lambda i: (jnp.minimum(i, nb1 - 1), 0)),
                     pl.BlockSpec((tm, D), lambda i: (jnp.maximum(i - nb1, 0), 0)))
    return pl.pallas_call(
        functools.partial(_combine_kernel, nc=nc, n_first_blocks=nb1), out_shape=out_shape, grid=(T // tm,),
        in_specs=[spec, yspec, yspec, pl.BlockSpec((tm, TOP_K), lambda i: (i, 0)),
                  pl.BlockSpec((1, D), lambda i: (0, 0))],
        out_specs=out_specs, scratch_shapes=[pltpu.VMEM((tm, D), F32)],
        compiler_params=_cparams("arbitrary"), name="moe_combine",
    )(x, y0, y1, gate, norm_w.reshape(1, D))


def _mm_kernel(*refs, n_terms, gated, residual):
    a_refs = refs[:n_terms]
    w_refs = refs[n_terms:2 * n_terms]
    pos = 2 * n_terms
    g_refs = refs[pos:pos + n_terms] if gated else ()
    pos += n_terms if gated else 0
    r_ref = refs[pos] if residual else None
    pos += 1 if residual else 0
    o_ref = refs[pos]
    wb_refs = refs[pos + 1:pos + 1 + n_terms]

    @pl.when(pl.program_id(1) == 0)
    def _():
        for w_ref, wb_ref in zip(w_refs, wb_refs):
            wb_ref[...] = w_ref[...].astype(BF16)

    acc = None
    for i in range(n_terms):
        y = _dot(a_refs[i][...], wb_refs[i][...])
        if gated:
            y = y * _sigmoid(g_refs[i][...])
        acc = y if acc is None else acc + y
    if residual:
        acc = acc + r_ref[...]
    o_ref[...] = acc.astype(o_ref.dtype)


def dense(a_list, w_list, layer, out_dtype, gate_src=None, gate_cols=None, residual=None, name="dense",
          tn=MM_TN, tm_max=640):
    n_terms = len(a_list)
    T = a_list[0].shape[0]
    N = w_list[0].shape[2]
    tm = _pick_tile(T, tuple(c for c in (640, 512, 320, 256, 128, 64, 32, 16, 8) if c <= tm_max))
    in_specs, args = [], []
    for a in a_list:
        in_specs.append(pl.BlockSpec((tm, a.shape[1]), lambda n, m: (m, 0)))
        args.append(a)
    for w in w_list:
        in_specs.append(pl.BlockSpec((None, w.shape[1], tn), lambda n, m: (layer, 0, n)))
        args.append(w)
    if gate_src is not None:
        for c0 in gate_cols:
            in_specs.append(pl.BlockSpec((tm, tn), functools.partial(lambda n, m, cb: (m, cb + n), cb=c0 // tn)))
            args.append(gate_src)
    if residual is not None:
        in_specs.append(pl.BlockSpec((tm, tn), lambda n, m: (m, n)))
        args.append(residual)
    kern = functools.partial(_mm_kernel, n_terms=n_terms, gated=gate_src is not None,
                             residual=residual is not None)
    return pl.pallas_call(
        kern,
        out_shape=jax.ShapeDtypeStruct((T, N), out_dtype),
        grid=(N // tn, T // tm),
        in_specs=in_specs,
        out_specs=pl.BlockSpec((tm, tn), lambda n, m: (m, n)),
        scratch_shapes=[pltpu.VMEM((w.shape[1], tn), BF16) for w in w_list],
        compiler_params=_cparams("arbitrary", "arbitrary"),
        name=name,
    )(*args)


def _conv_kernel(a_ref, b_ref, st_ref, w_ref, bdw_ref, lng_ref, lnb_ref, c_ref, ns_ref, ubuf, sh, cbuf, *, tl, rc):
    W1 = CONV_WIDTH - 1

    @pl.when(pl.program_id(1) == 0)
    def _():
        ubuf[0:W1, :] = st_ref[...]

    ubuf[W1:W1 + tl, :] = a_ref[...] * _sigmoid(b_ref[...])
    for r in range(1, SUBLANES):
        span = tl + SUBLANES * ((W1 - r) // SUBLANES)
        sh[r - 1, 0:span, :] = ubuf[r:r + span, :]
    dc = ubuf.shape[1]
    lc = LANES

    def chunk(i, carry):
        r0 = pl.multiple_of(i * rc, SUBLANES)
        for c0 in range(0, dc, lc):
            acc = jnp.zeros((rc, lc), F32)
            for j in range(CONV_WIDTH):
                q, r = divmod(j, SUBLANES)
                src = ubuf if r == 0 else sh.at[r - 1]
                acc = acc + src[pl.ds(r0 + SUBLANES * q, rc), c0:c0 + lc] * w_ref[j:j + 1, c0:c0 + lc]
            cbuf[pl.ds(r0, rc), c0:c0 + lc] = acc
        y = cbuf[pl.ds(r0, rc), :] + bdw_ref[...]
        mu = jnp.mean(y, axis=-1, keepdims=True)
        d = y - mu
        var = jnp.mean(d * d, axis=-1, keepdims=True)
        z = d * lax.rsqrt(var + EPS) * lng_ref[...] + lnb_ref[...]
        c_ref[pl.ds(r0, rc), :] = (z * _sigmoid(z)).astype(c_ref.dtype)
        return carry

    lax.fori_loop(0, tl // rc, chunk, 0)
    tail = ubuf[tl:tl + W1, :]
    ns_ref[...] = tail
    ubuf[0:W1, :] = tail


def conv_branch(proj, row0, B, L, state, w_dw, b_dw, ln_g, ln_b):
    dc = w_dw.shape[1]
    tl = _pick_tile(L, (512, 256, 128, 64, 32, 16))
    rc = min(tl, 64)
    nl = L // tl
    rb0 = row0 // tl
    kern = functools.partial(_conv_kernel, tl=tl, rc=rc)
    vec = pl.BlockSpec((1, dc), lambda b, l: (0, 0))
    return pl.pallas_call(
        kern,
        out_shape=(jax.ShapeDtypeStruct((B * L, dc), BF16),
                   jax.ShapeDtypeStruct((B, CONV_WIDTH - 1, dc), F32)),
        grid=(B, nl),
        in_specs=[pl.BlockSpec((tl, dc), lambda b, l: (rb0 + b * nl + l, 0)),
                  pl.BlockSpec((tl, dc), lambda b, l: (rb0 + b * nl + l, 1)),
                  pl.BlockSpec((None, CONV_WIDTH - 1, dc), lambda b, l: (b, 0, 0)),
                  pl.BlockSpec((CONV_WIDTH, dc), lambda b, l: (0, 0)), vec, vec, vec],
        out_specs=(pl.BlockSpec((tl, dc), lambda b, l: (b * nl + l, 0)),
                   pl.BlockSpec((None, CONV_WIDTH - 1, dc), lambda b, l: (b, 0, 0))),
        scratch_shapes=[pltpu.VMEM((tl + CONV_PAD, dc), F32),
                        pltpu.VMEM((SUBLANES - 1, tl + CONV_PAD - SUBLANES, dc), F32),
                        pltpu.VMEM((tl, dc), F32)],
        compiler_params=_cparams("arbitrary", "arbitrary"),
        name="conv_branch",
    )(proj, proj, state, w_dw, b_dw.reshape(1, dc), ln_g.reshape(1, dc), ln_b.reshape(1, dc))


def _pair_levels(C):
    t = np.arange(C)[:, None]
    s = np.arange(C)[None, :]
    x = np.bitwise_xor(t, s)
    lvl = np.where(x > 0, np.floor(np.log2(np.maximum(x, 1))).astype(np.int32) + 1, 0)
    return np.where(s <= t, lvl, -1).astype(np.int32)


def _group_boundary(b_ref, hs, m, C):
    if 2 * m >= SUBLANES:
        pieces = [jnp.broadcast_to(b_ref[pl.ds(g * 2 * m + m - 1, 1), hs], (2 * m, LANES))
                  for g in range(C // (2 * m))]
        return pieces[0] if len(pieces) == 1 else jnp.concatenate(pieces, axis=0)
    sub = lax.broadcasted_iota(jnp.int32, (SUBLANES, LANES), 0)
    pieces = []
    for v in range(C // SUBLANES):
        piece = jnp.broadcast_to(b_ref[pl.ds(v * SUBLANES + m - 1, 1), hs], (SUBLANES, LANES))
        for j in range(1, SUBLANES // (2 * m)):
            nxt = jnp.broadcast_to(b_ref[pl.ds(v * SUBLANES + j * 2 * m + m - 1, 1), hs], (SUBLANES, LANES))
            piece = jnp.where(sub >= j * 2 * m, nxt, piece)
        pieces.append(piece)
    return jnp.concatenate(pieces, axis=0)


def _hgrn_kernel(f_ref, i_ref, q_ref, g_ref, s0_ref, loglb_ref, log1m_ref, nw_ref, tri_ref, lvl_ref,
                 o_ref, sout_ref, st_ref, lf_ref, b_ref, *, C, precise):
    cidx = pl.program_id(1)
    prec = lax.Precision.HIGHEST if precise else None
    opd = (lambda a: a) if precise else (lambda a: a.astype(BF16))

    @pl.when(cidx == 0)
    def _():
        def init(h, carry):
            st_ref[h] = s0_ref[h].T
            return carry
        lax.fori_loop(0, HGRN_HEADS, init, 0)

    x = f_ref[...]
    log_sig = jnp.minimum(x, 0.0) - jnp.log(1.0 + jnp.exp(-jnp.abs(x)))
    la = loglb_ref[...]
    lc = log1m_ref[...] + log_sig
    lf = jnp.maximum(la, lc) + jnp.log(1.0 + jnp.exp(-jnp.abs(la - lc)))
    lf_ref[...] = lf
    hi = lf.astype(BF16)
    lo = (lf - hi.astype(F32)).astype(BF16)
    b_ref[...] = _dot(tri_ref[...], hi) + _dot(tri_ref[...], lo)

    row = lax.broadcasted_iota(jnp.int32, (C, LANES), 0)

    def head(h, carry):
        hs = pl.ds(pl.multiple_of(h * LANES, LANES), LANES)
        lf_h = lf_ref[:, hs]
        b = b_ref[:, hs]
        q = q_ref[:, hs]
        v = opd(i_ref[:, hs])
        k = 1.0 - jnp.exp(lf_h)
        lvl = lvl_ref[...]
        p = jnp.where(lvl == 0, _dot_nt(opd(q), opd(k), prec), 0.0)
        m, level = 1, 1
        while m < C:
            upper = (row & m) != 0
            if m == 1:
                xpo = jnp.where(upper, lf_h, 0.0)
            else:
                bnd = _group_boundary(b_ref, hs, m, C)
                xpo = jnp.where(upper, b - bnd, bnd - b)
            ex = jnp.exp(xpo)
            pm = _dot_nt(opd(q * ex), opd(k * ex), prec)
            p = jnp.where(lvl == level, pm, p)
            m *= 2
            level += 1
        s_t = st_ref[h]
        o = _dot(opd(p), v, prec) + _dot_nt(opd(q * jnp.exp(b)), opd(s_t), prec)
        b_last = b_ref[pl.ds(C - 1, 1), hs]
        k_dec = opd(k * jnp.exp(b_last - b))
        st_ref[h] = s_t * jnp.exp(b_last) + _dot_tn(v, k_dec, prec)
        ms = jnp.mean(o * o, axis=-1, keepdims=True)
        gv = g_ref[:, hs]
        o_ref[:, hs] = (o * lax.rsqrt(ms + EPS) * nw_ref[:, hs] * (gv * _sigmoid(gv))).astype(o_ref.dtype)
        return carry

    lax.fori_loop(0, HGRN_HEADS, head, 0, unroll=4)

    @pl.when(cidx == pl.num_programs(1) - 1)
    def _():
        def fin(h, carry):
            sout_ref[h] = st_ref[h].T
            return carry
        lax.fori_loop(0, HGRN_HEADS, fin, 0)


def hgrn_branch(proj, row0, B, L, s0, log_lb, log1m_lb, norm_w, col0, precise):
    dk = HGRN_HEADS * HGRN_DK
    C = _pick_tile(L, (HGRN_CHUNK, 64, 32, 16))
    nc = L // C
    rb0 = row0 // C
    cb0 = col0 // dk
    tri = jnp.asarray(np.tril(np.ones((C, C), np.float32)), BF16)
    lvl = jnp.asarray(_pair_levels(C))
    kern = functools.partial(_hgrn_kernel, C=C, precise=precise)

    def pspec(j):
        return pl.BlockSpec((C, dk), functools.partial(lambda b, c, j: (rb0 + b * nc + c, cb0 + j), j=j))

    vec = pl.BlockSpec((1, dk), lambda b, c: (0, 0))
    sspec = pl.BlockSpec((None, HGRN_HEADS, HGRN_DK, HGRN_DV), lambda b, c: (b, 0, 0, 0))
    return pl.pallas_call(
        kern,
        out_shape=(jax.ShapeDtypeStruct((B * L, dk), BF16),
                   jax.ShapeDtypeStruct((B, HGRN_HEADS, HGRN_DK, HGRN_DV), F32)),
        grid=(B, nc),
        in_specs=[pspec(0), pspec(1), pspec(2), pspec(3), sspec, vec, vec, vec,
                  pl.BlockSpec((C, C), lambda b, c: (0, 0)), pl.BlockSpec((C, C), lambda b, c: (0, 0))],
        out_specs=(pl.BlockSpec((C, dk), lambda b, c: (b * nc + c, 0)), sspec),
        scratch_shapes=[pltpu.VMEM((HGRN_HEADS, HGRN_DV, HGRN_DK), F32), pltpu.VMEM((C, dk), F32),
                        pltpu.VMEM((C, dk), F32)],
        compiler_params=_cparams("arbitrary", "arbitrary"),
        name="hgrn_branch",
    )(proj, proj, proj, proj, s0, log_lb.reshape(1, dk), log1m_lb.reshape(1, dk), norm_w.reshape(1, dk), tri, lvl)


def _attn_kernel(q_ref, k_ref, v_ref, o_ref, *, scale):
    hd = q_ref.shape[1] // MEM_HEADS
    for h in range(MEM_HEADS):
        cs = slice(h * hd, (h + 1) * hd)
        s = _dot_nt(q_ref[:, cs].astype(BF16), k_ref[:, cs].astype(BF16)) * scale
        s = s - jnp.max(s, axis=-1, keepdims=True)
        e = jnp.exp(s)
        pr = e / jnp.sum(e, axis=-1, keepdims=True)
        o_ref[:, cs] = _dot(pr.astype(BF16), v_ref[:, cs].astype(BF16)).astype(o_ref.dtype)


def attn_branch(proj, row0, B, L, mem_k, mem_v, col0):
    n_mem, dm = mem_k.shape[1], mem_k.shape[2]
    tl = _pick_tile(L, (512, 256, 128, 64, 32, 16))
    nl = L // tl
    rb0 = row0 // tl
    cb0 = col0 // dm
    kern = functools.partial(_attn_kernel, scale=float((dm // MEM_HEADS) ** -0.5))
    mspec = pl.BlockSpec((None, n_mem, dm), lambda b, l: (b, 0, 0))
    return pl.pallas_call(
        kern,
        out_shape=jax.ShapeDtypeStruct((B * L, dm), BF16),
        grid=(B, nl),
        in_specs=[pl.BlockSpec((tl, dm), lambda b, l: (rb0 + b * nl + l, cb0)), mspec, mspec],
        out_specs=pl.BlockSpec((tl, dm), lambda b, l: (b * nl + l, 0)),
        compiler_params=_cparams("parallel", "parallel"),
        name="attn_branch",
    )(proj, mem_k, mem_v)


def _moe_kernel(be_ref, nr_ref, nv_ref, x_ref, win_ref, wout_ref, o_ref, acc_ref, hm_ref, *, n1, sb):
    b = pl.program_id(0)
    s = pl.program_id(1)
    nr = nr_ref[b]
    R = x_ref.shape[0]
    F, nc = wout_ref.shape
    nsub = (nr + sb - 1) // sb

    @pl.when(s < n1)
    def _():
        shift = ((s % 2) * 16).astype(jnp.uint32)
        for k in range(1, R // sb + 1):
            rows = slice(0, k * sb)

            @pl.when(nsub == k)
            def _():
                xk = lax.bitcast_convert_type((x_ref[rows, :] << shift) & jnp.uint32(0xFFFF0000), F32).astype(BF16)
                part = _dot(xk, win_ref[...].astype(BF16))

                @pl.when(s == 0)
                def _():
                    acc_ref[rows, :] = part

                @pl.when(s > 0)
                def _():
                    acc_ref[rows, :] += part

                @pl.when(s == n1 - 1)
                def _():
                    a = acc_ref[rows, :F]
                    hm_ref[rows, :] = (a * _sigmoid(a) * acc_ref[rows, F:]).astype(BF16)

    @pl.when(s >= n1)
    def _():
        for k in range(0, R // sb + 1):
            @pl.when(nsub == k)
            def _():
                if k > 0:
                    rows = slice(0, k * sb)
                    o_ref[rows, :] = _pack_pairs(_dot(hm_ref[rows, :], wout_ref[...].astype(BF16)))
                if k < R // sb:
                    o_ref[k * sb:, :] = jnp.zeros((R - k * sb, nc // 2), jnp.uint32)


def moe_experts(xs, blk_e, blk_rows, n_valid, w_ein, w_eout, layer):
    P = xs.shape[0]
    D, F2 = w_ein.shape[2], w_ein.shape[3]
    F = w_eout.shape[2]
    R = MOE_ROWS
    nb = P // R
    n1 = 4
    kc = D // n1
    nc = MOE_NC
    n2 = D // nc

    def x_chunk(b, s, be, nr, nv):
        return (jnp.minimum(b, nv[0] - 1), jnp.where(b < nv[0], jnp.minimum(s // 2, n1 // 2 - 1), n1 // 2 - 1))

    def in_chunk(b, s, be, nr, nv):
        sc = jnp.minimum(s, n1 - 1)
        return (layer, be[b], jnp.where(b < nv[0], (sc % 2) * (n1 // 2) + sc // 2, n1 - 1), 0)

    def out_chunk(b, s, be, nr, nv):
        early = s < 2
        e = jnp.where(early, be[jnp.maximum(b - 1, 0)], be[b])
        col = jnp.where(jnp.logical_or(early, b >= nv[0]), n2 - 1, jnp.maximum(s - n1, 0))
        return (layer, e, 0, col)

    def y_chunk(b, s, be, nr, nv):
        spare = b > nv[0]
        return (jnp.where(spare, nv[0], b), jnp.where(spare, n2 - 1, jnp.maximum(s - n1, 0)))

    return pl.pallas_call(
        functools.partial(_moe_kernel, n1=n1, sb=MOE_SUB),
        out_shape=jax.ShapeDtypeStruct((P, D // 2), jnp.uint32),
        grid_spec=pltpu.PrefetchScalarGridSpec(
            num_scalar_prefetch=3, grid=(nb, n1 + n2),
            in_specs=[pl.BlockSpec((R, kc), x_chunk),
                      pl.BlockSpec((None, None, kc, F2), in_chunk),
                      pl.BlockSpec((None, None, F, nc), out_chunk)],
            out_specs=pl.BlockSpec((R, nc // 2), y_chunk),
            scratch_shapes=[pltpu.VMEM((R, F2), F32), pltpu.VMEM((R, F), BF16)]),
        compiler_params=_cparams("arbitrary", "arbitrary"),
        name="moe_experts",
    )(blk_e, blk_rows, n_valid, xs, w_ein, w_eout)


def _route(logits, b_rg, b_re):
    T = logits.shape[0]
    g_logits = logits[:, :N_GROUPS] + b_rg
    g_prob = jax.nn.softmax(g_logits, axis=-1)
    grp = jnp.argmax(g_logits, axis=-1).astype(jnp.int32)
    p_grp = jnp.take_along_axis(g_prob, grp[:, None], axis=-1)
    e_logits = (logits[:, N_GROUPS:N_GROUPS + N_EXPERTS] + b_re).reshape(T, N_GROUPS, EXPERTS_PER_GROUP)
    e_sel = jnp.take_along_axis(e_logits, grp[:, None, None], axis=1)[:, 0]
    top_val, top_idx = lax.top_k(e_sel, TOP_K)
    gate = p_grp * jax.nn.softmax(top_val, axis=-1)
    eid = grp[:, None] * EXPERTS_PER_GROUP + top_idx
    return eid, gate


def hier_moe(x, norm_w, w_rg, b_rg, w_re, b_re, w_ein, w_eout, layer, next_norm_w, n_first):
    T, D = x.shape
    w_router = jnp.concatenate([w_rg, w_re, jnp.zeros((D, LANES - N_GROUPS - N_EXPERTS), F32)], axis=1)
    h, logits = rmsnorm_router(x, norm_w, w_router)
    eid, gate = _route(logits, b_rg, b_re)
    R = MOE_ROWS
    A = T * TOP_K
    flat_e = eid.reshape(-1)
    onehot = (flat_e[:, None] == jnp.arange(N_EXPERTS, dtype=jnp.int32)[None, :]).astype(jnp.int32)
    csum = jnp.cumsum(onehot, axis=0)
    rank = jnp.take_along_axis(csum, flat_e[:, None], axis=1)[:, 0] - 1
    counts = csum[-1]
    padded = (counts + R - 1) // R * R
    pend = jnp.cumsum(padded)
    pstart = pend - padded
    dest = pstart[flat_e] + rank
    nb = -(-(A + N_EXPERTS * (R - 1)) // R)
    P = nb * R
    flat_tok = jnp.repeat(jnp.arange(T, dtype=jnp.int32), TOP_K)
    buf_tok = (jnp.arange(P, dtype=jnp.int32) % T).at[dest].set(flat_tok)
    blk = jnp.arange(nb, dtype=jnp.int32)
    n_valid = (pend[-1] // R).astype(jnp.int32)
    blk_raw = jnp.minimum(jnp.sum(blk[:, None] * R >= pend[None, :], axis=1), N_EXPERTS - 1).astype(jnp.int32)
    live = blk < n_valid
    blk_e = jnp.where(live, blk_raw, blk_raw[jnp.maximum(n_valid - 1, 0)])
    blk_rows = jnp.where(live, jnp.clip(counts[blk_raw] - (blk * R - pstart[blk_raw]), 0, R), 0).astype(jnp.int32)
    xs = jnp.take(h, buf_tok, axis=0, mode="clip")
    y = moe_experts(xs, blk_e, blk_rows, n_valid.reshape(1), w_ein, w_eout, layer)
    dest2 = dest.reshape(T, TOP_K)
    y0 = jnp.take(y, dest2[:, 0], axis=0, mode="clip")
    y1 = jnp.take(y, dest2[:, 1], axis=0, mode="clip")
    return moe_combine(x, y0, y1, gate, next_norm_w, MOE_NC, n_first)


def _layer(x, h, groups, log_lb, log1m_lb, p, layer, next_norm_w, n_first):
    dc = p['w_dw'].shape[1]
    dk = HGRN_HEADS * HGRN_DK
    dm = p['w_mem_out'].shape[1]
    D = x.shape[1]
    proj = dense([h], [p['w_in']], layer, F32, name="in_proj", tn=1024, tm_max=320)
    col_hgrn = 2 * dc
    col_mem = col_hgrn + 4 * dk
    col_gate = col_mem + dm
    cs, os_, oms, conv_states, hgrn_states = [], [], [], [], []
    for row0, B, L, conv_state, s0, mem_k, mem_v in groups:
        c, ns = conv_branch(proj, row0, B, L, conv_state, p['w_dw'], p['b_dw'], p['conv_ln_g'], p['conv_ln_b'])
        o, s_new = hgrn_branch(proj, row0, B, L, s0, log_lb, log1m_lb, p['hgrn_norm'], col_hgrn,
                               precise=L <= HGRN_PRECISE_MAX_LEN)
        om = attn_branch(proj, row0, B, L, mem_k.reshape(B, -1, dm), mem_v.reshape(B, -1, dm), col_mem)
        cs.append(c); os_.append(o); oms.append(om); conv_states.append(ns); hgrn_states.append(s_new)
    cat = lambda xs: xs[0] if len(xs) == 1 else jnp.concatenate(xs, axis=0)
    merged = dense([cat(cs), cat(os_), cat(oms)], [p['w_conv_out'], p['w_hgrn_out'], p['w_mem_out']], layer, BF16,
                   gate_src=proj, gate_cols=[col_gate, col_gate + D, col_gate + 2 * D], name="branch_merge")
    x = dense([merged], [p['w_out']], layer, F32, residual=x, name="out_proj", tn=1024, tm_max=320)
    outs = hier_moe(x, p['norm_ffn'], p['w_router_group'], p['b_router_group'], p['w_router_expert'],
                    p['b_router_expert'], p['w_exp_in'], p['w_exp_out'], layer, next_norm_w, n_first)
    return outs, conv_states, hgrn_states


def kernel(x_prompt, x_sample, state_conv, state_hgrn, cache_mem_k, cache_mem_v, mem_prompt, norm_mix, norm_mem, norm_ffn, norm_final, w_in, w_dw, b_dw, conv_ln_g, conv_ln_b, w_conv_out, hgrn_lb, hgrn_norm, w_hgrn_out, w_mem_kv, w_mem_out, w_out, w_router_group, b_router_group, w_router_expert, b_router_expert, w_exp_in, w_exp_out):
    depth = w_in.shape[0]
    Bp, Lp, D = x_prompt.shape
    Bs, Ls, _ = x_sample.shape
    n_mem = mem_prompt.shape[1]
    dc = w_dw.shape[2]
    dm = w_mem_out.shape[1]
    lb_all = jnp.cumsum(jax.nn.softmax(hgrn_lb.astype(F32), axis=0), axis=0)
    lb_all = lb_all - lb_all[:1]
    log_lb = jnp.log(lb_all)
    log1m_lb = jnp.log1p(-lb_all)
    x = jnp.concatenate([x_prompt.reshape(Bp * Lp, D), x_sample.reshape(Bs * Ls, D)], axis=0)
    h = rmsnorm(x, norm_mix[0], BF16)
    conv0 = jnp.zeros((Bp, CONV_WIDTH - 1, dc), F32)
    s0 = jnp.zeros((Bp, HGRN_HEADS, HGRN_DK, HGRN_DV), F32)
    mem_flat = mem_prompt.reshape(Bp * n_mem, D)
    conv_p, hgrn_p, mk_p, mv_p, conv_s, hgrn_s = [], [], [], [], [], []
    for l in range(depth):
        p = dict(norm_mix=norm_mix[l], w_in=w_in, w_dw=w_dw[l], b_dw=b_dw[l], conv_ln_g=conv_ln_g[l],
                 conv_ln_b=conv_ln_b[l], w_conv_out=w_conv_out, hgrn_norm=hgrn_norm[l], w_hgrn_out=w_hgrn_out,
                 w_mem_out=w_mem_out, w_out=w_out, norm_ffn=norm_ffn[l], w_router_group=w_router_group[l],
                 b_router_group=b_router_group[l], w_router_expert=w_router_expert[l],
                 b_router_expert=b_router_expert[l], w_exp_in=w_exp_in, w_exp_out=w_exp_out)
        kv = dense([rmsnorm(mem_flat, norm_mem[l], BF16)], [w_mem_kv], l, F32, name="mem_kv")
        mk = kv[:, :dm].reshape(Bp, n_mem, MEM_HEADS, dm // MEM_HEADS)
        mv = kv[:, dm:].reshape(Bp, n_mem, MEM_HEADS, dm // MEM_HEADS)
        groups = [(0, Bp, Lp, conv0, s0, mk, mv),
                  (Bp * Lp, Bs, Ls, state_conv[l], state_hgrn[l], cache_mem_k[l], cache_mem_v[l])]
        last = l == depth - 1
        outs, cstates, hstates = _layer(x, h, groups, log_lb[l], log1m_lb[l], p, l,
                                        norm_final if last else norm_mix[l + 1], Bp * Lp if last else None)
        if not last:
            x, h = outs
        conv_p.append(cstates[0]); hgrn_p.append(hstates[0]); mk_p.append(mk); mv_p.append(mv)
        conv_s.append(cstates[1]); hgrn_s.append(hstates[1])
    y_prompt = outs[0].reshape(Bp, Lp, D)
    y_sample = outs[1].reshape(Bs, Ls, D)
    return (y_prompt, y_sample, jnp.stack(conv_p), jnp.stack(hgrn_p), jnp.stack(mk_p), jnp.stack(mv_p),
            jnp.stack(conv_s), jnp.stack(hgrn_s))
```

```python
import functools

import numpy as np
import jax
import jax.numpy as jnp
from jax import lax
from jax.experimental import pallas as pl
from jax.experimental.pallas import tpu as pltpu

F32 = jnp.float32
BF16 = jnp.bfloat16

EPS = 1e-6
CONV_WIDTH = 31
HGRN_HEADS = 16
HGRN_DK = 128
HGRN_DV = 128
MEM_HEADS = 4
N_GROUPS = 4
EXPERTS_PER_GROUP = 8
N_EXPERTS = N_GROUPS * EXPERTS_PER_GROUP
TOP_K = 2

LANES = 128
SUBLANES = 8
VMEM_LIMIT_BYTES = 56 * 1024 * 1024
MM_TN = 512
MOE_ROWS = 640
MOE_NC = 1024
MOE_SUB = 128
CONV_PAD = 32
HGRN_CHUNK = 128
HGRN_PRECISE_MAX_LEN = 64


def _cparams(*sem):
    return pltpu.CompilerParams(dimension_semantics=sem, vmem_limit_bytes=VMEM_LIMIT_BYTES)


def _pick_tile(n, cands):
    for c in cands:
        if n % c == 0:
            return c
    return n


def _sigmoid(x):
    return 1.0 / (1.0 + jnp.exp(-x))


def _dot(a, b, precision=None):
    return jnp.dot(a, b, preferred_element_type=F32, precision=precision)


def _dot_nt(a, b, precision=None):
    return lax.dot_general(a, b, (((1,), (1,)), ((), ())), preferred_element_type=F32, precision=precision)


def _dot_tn(a, b, precision=None):
    return lax.dot_general(a, b, (((0,), (0,)), ((), ())), preferred_element_type=F32, precision=precision)


def _rms_body(x, g):
    ms = jnp.mean(x * x, axis=-1, keepdims=True)
    return x * lax.rsqrt(ms + EPS) * g


def _rms_kernel(x_ref, g_ref, o_ref):
    o_ref[...] = _rms_body(x_ref[...], g_ref[...]).astype(o_ref.dtype)


def rmsnorm(x, g, out_dtype):
    T, D = x.shape
    tm = _pick_tile(T, (320, 256, 128, 64, 32, 16, 8))
    return pl.pallas_call(
        _rms_kernel,
        out_shape=jax.ShapeDtypeStruct((T, D), out_dtype),
        grid=(T // tm,),
        in_specs=[pl.BlockSpec((tm, D), lambda i: (i, 0)), pl.BlockSpec((1, D), lambda i: (0, 0))],
        out_specs=pl.BlockSpec((tm, D), lambda i: (i, 0)),
        compiler_params=_cparams("parallel"),
        name="rmsnorm",
    )(x, g.reshape(1, D))


def _pack_pairs(h):
    half = h.shape[1] // 2
    bits = lax.bitcast_convert_type(h.astype(BF16).astype(F32), jnp.uint32)
    return bits[:, :half] | (bits[:, half:] >> 16)


def _unpack_pairs_f32(w):
    hi = lax.bitcast_convert_type(w & jnp.uint32(0xFFFF0000), F32)
    lo = lax.bitcast_convert_type(w << 16, F32)
    return hi, lo


ROUTE_E0, ROUTE_RANK0, ROUTE_GATE0 = 0, 2, 4


def _rms_router_kernel(x_ref, g_ref, wr_ref, br_ref, tri_ref, h_ref, rt_ref, cnt_ref, run_ref):
    @pl.when(pl.program_id(0) == 0)
    def _():
        run_ref[...] = jnp.zeros_like(run_ref)

    h = _rms_body(x_ref[...], g_ref[...])
    h_ref[...] = _pack_pairs(h)
    lg = jnp.dot(h, wr_ref[...], precision=lax.Precision.HIGHEST, preferred_element_type=F32) + br_ref[...]
    lane = lax.broadcasted_iota(jnp.int32, lg.shape, 1)
    lanef = lane.astype(F32)
    neg, far = -3.0e38, float(LANES)

    def first_max(v):
        m = jnp.max(v, axis=-1, keepdims=True)
        return m, jnp.min(jnp.where(v == m, lanef, far), axis=-1, keepdims=True)

    is_grp = lane < N_GROUPS
    gmax, grp = first_max(jnp.where(is_grp, lg, neg))
    p_grp = 1.0 / jnp.sum(jnp.where(is_grp, jnp.exp(lg - gmax), 0.0), axis=-1, keepdims=True)
    egrp = jnp.right_shift(lane - N_GROUPS, EXPERTS_PER_GROUP.bit_length() - 1).astype(F32)
    in_grp = (lane >= N_GROUPS) & (lane < N_GROUPS + N_EXPERTS) & (egrp == grp)
    el = jnp.where(in_grp, lg, neg)
    v1, i1 = first_max(el)
    v2, i2 = first_max(jnp.where(lanef == i1, neg, el))
    e2 = jnp.exp(v2 - v1)
    gate1 = p_grp * (1.0 / (1.0 + e2))
    gate2 = p_grp * (e2 / (1.0 + e2))
    hit1, hit2 = lanef == i1, lanef == i2
    chosen = jnp.where(hit1 | hit2, 1.0, 0.0)
    before = _dot(tri_ref[...], chosen.astype(BF16)) + run_ref[...]
    rank1 = jnp.sum(jnp.where(hit1, before, 0.0), axis=-1, keepdims=True)
    rank2 = jnp.sum(jnp.where(hit2, before, 0.0), axis=-1, keepdims=True)
    run_ref[...] += jnp.sum(chosen, axis=0, keepdims=True)
    cnt_ref[...] = run_ref[...]
    rt = jnp.where(lane == ROUTE_E0, i1 - N_GROUPS, jnp.where(lane == ROUTE_E0 + 1, i2 - N_GROUPS, 0.0))
    rt = jnp.where(lane == ROUTE_RANK0, rank1, jnp.where(lane == ROUTE_RANK0 + 1, rank2, rt))
    rt_ref[...] = jnp.where(lane == ROUTE_GATE0, gate1, jnp.where(lane == ROUTE_GATE0 + 1, gate2, rt))


def rmsnorm_router(x, g, w_router, b_router):
    T, D = x.shape
    tm = _pick_tile(T, (320, 256, 128, 64, 32, 16, 8))
    tri = jnp.asarray(np.tril(np.ones((tm, tm), np.float32), -1), BF16)
    full = lambda shape: pl.BlockSpec(shape, lambda i: (0, 0))
    return pl.pallas_call(
        _rms_router_kernel,
        out_shape=(jax.ShapeDtypeStruct((T, D // 2), jnp.uint32), jax.ShapeDtypeStruct((T, LANES), F32),
                   jax.ShapeDtypeStruct((1, LANES), F32)),
        grid=(T // tm,),
        in_specs=[pl.BlockSpec((tm, D), lambda i: (i, 0)), full((1, D)), full((D, LANES)), full((1, LANES)),
                  full((tm, tm))],
        out_specs=(pl.BlockSpec((tm, D // 2), lambda i: (i, 0)), pl.BlockSpec((tm, LANES), lambda i: (i, 0)),
                   full((1, LANES))),
        scratch_shapes=[pltpu.VMEM((1, LANES), F32)],
        compiler_params=_cparams("arbitrary"),
        name="rmsnorm_router",
    )(x, g.reshape(1, D), w_router, b_router, tri)


def _combine_kernel(x_ref, y0_ref, y1_ref, g_ref, nw_ref, o1_ref, o2_ref, xbuf, *, nc, n_first_blocks):
    D = x_ref.shape[1]
    g = g_ref[...]
    g0, g1 = g[:, 0:1], g[:, 1:2]
    ss = jnp.zeros((x_ref.shape[0], 1), F32)
    for n in range(D // nc):
        cw = slice(n * (nc // 2), (n + 1) * (nc // 2))
        hi0, lo0 = _unpack_pairs_f32(y0_ref[:, cw])
        hi1, lo1 = _unpack_pairs_f32(y1_ref[:, cw])
        ca = slice(n * nc, n * nc + nc // 2)
        cb = slice(n * nc + nc // 2, (n + 1) * nc)
        xa = x_ref[:, ca] + (hi0 * g0 + hi1 * g1)
        xb = x_ref[:, cb] + (lo0 * g0 + lo1 * g1)
        xbuf[:, ca] = xa
        xbuf[:, cb] = xb
        ss = ss + jnp.sum(xa * xa, axis=-1, keepdims=True) + jnp.sum(xb * xb, axis=-1, keepdims=True)
    xn = xbuf[...]
    y = xn * lax.rsqrt(ss * (1.0 / D) + EPS) * nw_ref[...]
    if n_first_blocks is None:
        o1_ref[...] = xn
        o2_ref[...] = y.astype(o2_ref.dtype)
    else:
        i = pl.program_id(0)

        @pl.when(i < n_first_blocks)
        def _():
            o1_ref[...] = y

        @pl.when(i >= n_first_blocks)
        def _():
            o2_ref[...] = y


def moe_combine(x, y0, y1, gate, norm_w, nc, n_first=None):
    T, D = x.shape
    tm = _pick_tile(T if n_first is None else int(np.gcd(n_first, T - n_first)), (128, 64, 32, 16, 8))
    spec = pl.BlockSpec((tm, D), lambda i: (i, 0))
    yspec = pl.BlockSpec((tm, D // 2), lambda i: (i, 0))
    if n_first is None:
        nb1 = None
        out_shape = (jax.ShapeDtypeStruct((T, D), F32), jax.ShapeDtypeStruct((T, D), BF16))
        out_specs = (spec, spec)
    else:
        nb1 = n_first // tm
        out_shape = (jax.ShapeDtypeStruct((n_first, D), F32), jax.ShapeDtypeStruct((T - n_first, D), F32))
        out_specs = (pl.BlockSpec((tm, D), lambda i: (jnp.minimum(i, nb1 - 1), 0)),
                     pl.BlockSpec((tm, D), lambda i: (jnp.maximum(i - nb1, 0), 0)))
    return pl.pallas_call(
        functools.partial(_combine_kernel, nc=nc, n_first_blocks=nb1), out_shape=out_shape, grid=(T // tm,),
        in_specs=[spec, yspec, yspec, pl.BlockSpec((tm, TOP_K), lambda i: (i, 0)),
                  pl.BlockSpec((1, D), lambda i: (0, 0))],
        out_specs=out_specs, scratch_shapes=[pltpu.VMEM((tm, D), F32)],
        compiler_params=_cparams("arbitrary"), name="moe_combine",
    )(x, y0, y1, gate, norm_w.reshape(1, D))


def _mm_kernel(*refs, n_terms, gated, residual, n_lo):
    a_refs = refs[:n_terms]
    w_refs = refs[n_terms:2 * n_terms]
    pos = 2 * n_terms
    g_refs = refs[pos:pos + n_terms] if gated else ()
    pos += n_terms if gated else 0
    r_ref = refs[pos] if residual else None
    pos += 1 if residual else 0
    o_ref = refs[pos]
    pos += 1
    prob_ref = None if n_lo is None else refs[pos]
    pos += 0 if n_lo is None else 1
    wb_refs = refs[pos:pos + n_terms]

    @pl.when(pl.program_id(1) == 0)
    def _():
        for w_ref, wb_ref in zip(w_refs, wb_refs):
            wb_ref[...] = w_ref[...].astype(BF16)

    acc = None
    for i in range(n_terms):
        y = _dot(a_refs[i][...], wb_refs[i][...])
        if gated:
            y = y * g_refs[i][...]
        acc = y if acc is None else acc + y
    if residual:
        acc = acc + r_ref[...]
    if n_lo is None:
        o_ref[...] = acc.astype(o_ref.dtype)
    else:
        n = pl.program_id(0)

        @pl.when(n < n_lo)
        def _():
            o_ref[...] = acc.astype(o_ref.dtype)

        @pl.when(n >= n_lo)
        def _():
            prob_ref[...] = _sigmoid(acc).astype(prob_ref.dtype)


def dense(a_list, w_list, layer, out_dtype, gate_src=None, gate_cols=None, residual=None, name="dense",
          tn=MM_TN, tm_max=640, prob_from=None):
    n_terms = len(a_list)
    T = a_list[0].shape[0]
    N = w_list[0].shape[2]
    tm = _pick_tile(T, tuple(c for c in (640, 512, 320, 256, 128, 64, 32, 16, 8) if c <= tm_max))
    in_specs, args = [], []
    for a in a_list:
        in_specs.append(pl.BlockSpec((tm, a.shape[1]), lambda n, m: (m, 0)))
        args.append(a)
    for w in w_list:
        in_specs.append(pl.BlockSpec((None, w.shape[1], tn), lambda n, m: (layer, 0, n)))
        args.append(w)
    if gate_src is not None:
        for c0 in gate_cols:
            in_specs.append(pl.BlockSpec((tm, tn), functools.partial(lambda n, m, cb: (m, cb + n), cb=c0 // tn)))
            args.append(gate_src)
    if residual is not None:
        in_specs.append(pl.BlockSpec((tm, tn), lambda n, m: (m, n)))
        args.append(residual)
    n_lo = None if prob_from is None else prob_from // tn
    kern = functools.partial(_mm_kernel, n_terms=n_terms, gated=gate_src is not None,
                             residual=residual is not None, n_lo=n_lo)
    if n_lo is None:
        out_shape = jax.ShapeDtypeStruct((T, N), out_dtype)
        out_specs = pl.BlockSpec((tm, tn), lambda n, m: (m, n))
    else:
        m_last = T // tm - 1
        out_shape = (jax.ShapeDtypeStruct((T, prob_from), out_dtype), jax.ShapeDtypeStruct((T, N - prob_from), BF16))
        out_specs = (pl.BlockSpec((tm, tn), lambda n, m: (jnp.where(n < n_lo, m, m_last), jnp.minimum(n, n_lo - 1))),
                     pl.BlockSpec((tm, tn), lambda n, m: (jnp.where(n >= n_lo, m, 0), jnp.maximum(n - n_lo, 0))))
    return pl.pallas_call(
        kern,
        out_shape=out_shape,
        grid=(N // tn, T // tm),
        in_specs=in_specs,
        out_specs=out_specs,
        scratch_shapes=[pltpu.VMEM((w.shape[1], tn), BF16) for w in w_list],
        compiler_params=_cparams("arbitrary", "arbitrary"),
        name=name,
    )(*args)


def _conv_kernel(a_ref, b_ref, st_ref, w_ref, bdw_ref, lng_ref, lnb_ref, c_ref, ns_ref, ubuf, sh, cbuf, *, tl, rc):
    W1 = CONV_WIDTH - 1

    @pl.when(pl.program_id(1) == 0)
    def _():
        ubuf[0:W1, :] = st_ref[...]

    ubuf[W1:W1 + tl, :] = a_ref[...] * _sigmoid(b_ref[...])
    for r in range(1, SUBLANES):
        span = tl + SUBLANES * ((W1 - r) // SUBLANES)
        sh[r - 1, 0:span, :] = ubuf[r:r + span, :]
    dc = ubuf.shape[1]
    lc = LANES

    def chunk(i, carry):
        r0 = pl.multiple_of(i * rc, SUBLANES)
        for c0 in range(0, dc, lc):
            acc = jnp.zeros((rc, lc), F32)
            for j in range(CONV_WIDTH):
                q, r = divmod(j, SUBLANES)
                src = ubuf if r == 0 else sh.at[r - 1]
                acc = acc + src[pl.ds(r0 + SUBLANES * q, rc), c0:c0 + lc] * w_ref[j:j + 1, c0:c0 + lc]
            cbuf[pl.ds(r0, rc), c0:c0 + lc] = acc
        y = cbuf[pl.ds(r0, rc), :] + bdw_ref[...]
        mu = jnp.mean(y, axis=-1, keepdims=True)
        d = y - mu
        var = jnp.mean(d * d, axis=-1, keepdims=True)
        z = d * lax.rsqrt(var + EPS) * lng_ref[...] + lnb_ref[...]
        c_ref[pl.ds(r0, rc), :] = (z * _sigmoid(z)).astype(c_ref.dtype)
        return carry

    lax.fori_loop(0, tl // rc, chunk, 0)
    tail = ubuf[tl:tl + W1, :]
    ns_ref[...] = tail
    ubuf[0:W1, :] = tail


def conv_branch(proj, row0, B, L, state, w_dw, b_dw, ln_g, ln_b):
    dc = w_dw.shape[1]
    tl = _pick_tile(L, (512, 256, 128, 64, 32, 16))
    rc = min(tl, 64)
    nl = L // tl
    rb0 = row0 // tl
    kern = functools.partial(_conv_kernel, tl=tl, rc=rc)
    vec = pl.BlockSpec((1, dc), lambda b, l: (0, 0))
    return pl.pallas_call(
        kern,
        out_shape=(jax.ShapeDtypeStruct((B * L, dc), BF16),
                   jax.ShapeDtypeStruct((B, CONV_WIDTH - 1, dc), F32)),
        grid=(B, nl),
        in_specs=[pl.BlockSpec((tl, dc), lambda b, l: (rb0 + b * nl + l, 0)),
                  pl.BlockSpec((tl, dc), lambda b, l: (rb0 + b * nl + l, 1)),
                  pl.BlockSpec((None, CONV_WIDTH - 1, dc), lambda b, l: (b, 0, 0)),
                  pl.BlockSpec((CONV_WIDTH, dc), lambda b, l: (0, 0)), vec, vec, vec],
        out_specs=(pl.BlockSpec((tl, dc), lambda b, l: (b * nl + l, 0)),
                   pl.BlockSpec((None, CONV_WIDTH - 1, dc), lambda b, l: (b, 0, 0))),
        scratch_shapes=[pltpu.VMEM((tl + CONV_PAD, dc), F32),
                        pltpu.VMEM((SUBLANES - 1, tl + CONV_PAD - SUBLANES, dc), F32),
                        pltpu.VMEM((tl, dc), F32)],
        compiler_params=_cparams("arbitrary", "arbitrary"),
        name="conv_branch",
    )(proj, proj, state, w_dw, b_dw.reshape(1, dc), ln_g.reshape(1, dc), ln_b.reshape(1, dc))


def _pair_levels(C):
    t = np.arange(C)[:, None]
    s = np.arange(C)[None, :]
    x = np.bitwise_xor(t, s)
    lvl = np.where(x > 0, np.floor(np.log2(np.maximum(x, 1))).astype(np.int32) + 1, 0)
    return np.where(s <= t, lvl, -1).astype(np.int32)


def _group_boundary(b_ref, hs, m, C):
    if 2 * m >= SUBLANES:
        pieces = [jnp.broadcast_to(b_ref[pl.ds(g * 2 * m + m - 1, 1), hs], (2 * m, LANES))
                  for g in range(C // (2 * m))]
        return pieces[0] if len(pieces) == 1 else jnp.concatenate(pieces, axis=0)
    sub = lax.broadcasted_iota(jnp.int32, (SUBLANES, LANES), 0)
    pieces = []
    for v in range(C // SUBLANES):
        piece = jnp.broadcast_to(b_ref[pl.ds(v * SUBLANES + m - 1, 1), hs], (SUBLANES, LANES))
        for j in range(1, SUBLANES // (2 * m)):
            nxt = jnp.broadcast_to(b_ref[pl.ds(v * SUBLANES + j * 2 * m + m - 1, 1), hs], (SUBLANES, LANES))
            piece = jnp.where(sub >= j * 2 * m, nxt, piece)
        pieces.append(piece)
    return jnp.concatenate(pieces, axis=0)


def _hgrn_kernel(f_ref, i_ref, q_ref, g_ref, s0_ref, loglb_ref, log1m_ref, nw_ref, tri_ref, lvl_ref,
                 o_ref, sout_ref, st_ref, lf_ref, b_ref, *, C, precise):
    cidx = pl.program_id(1)
    prec = lax.Precision.HIGHEST if precise else None
    opd = (lambda a: a) if precise else (lambda a: a.astype(BF16))

    @pl.when(cidx == 0)
    def _():
        def init(h, carry):
            st_ref[h] = s0_ref[h].T
            return carry
        lax.fori_loop(0, HGRN_HEADS, init, 0)

    x = f_ref[...]
    log_sig = jnp.minimum(x, 0.0) - jnp.log(1.0 + jnp.exp(-jnp.abs(x)))
    la = loglb_ref[...]
    lc = log1m_ref[...] + log_sig
    lf = jnp.maximum(la, lc) + jnp.log(1.0 + jnp.exp(-jnp.abs(la - lc)))
    lf_ref[...] = lf
    hi = lf.astype(BF16)
    lo = (lf - hi.astype(F32)).astype(BF16)
    b_ref[...] = _dot(tri_ref[...], hi) + _dot(tri_ref[...], lo)

    row = lax.broadcasted_iota(jnp.int32, (C, LANES), 0)

    def head(h, carry):
        hs = pl.ds(pl.multiple_of(h * LANES, LANES), LANES)
        lf_h = lf_ref[:, hs]
        b = b_ref[:, hs]
        q = q_ref[:, hs]
        v = opd(i_ref[:, hs])
        k = 1.0 - jnp.exp(lf_h)
        lvl = lvl_ref[...]
        p = jnp.where(lvl == 0, _dot_nt(opd(q), opd(k), prec), 0.0)
        m, level = 1, 1
        while m < C:
            upper = (row & m) != 0
            if m == 1:
                xpo = jnp.where(upper, lf_h, 0.0)
            else:
                bnd = _group_boundary(b_ref, hs, m, C)
                xpo = jnp.where(upper, b - bnd, bnd - b)
            ex = jnp.exp(xpo)
            pm = _dot_nt(opd(q * ex), opd(k * ex), prec)
            p = jnp.where(lvl == level, pm, p)
            m *= 2
            level += 1
        s_t = st_ref[h]
        o = _dot(opd(p), v, prec) + _dot_nt(opd(q * jnp.exp(b)), opd(s_t), prec)
        b_last = b_ref[pl.ds(C - 1, 1), hs]
        k_dec = opd(k * jnp.exp(b_last - b))
        st_ref[h] = s_t * jnp.exp(b_last) + _dot_tn(v, k_dec, prec)
        ms = jnp.mean(o * o, axis=-1, keepdims=True)
        gv = g_ref[:, hs]
        o_ref[:, hs] = (o * lax.rsqrt(ms + EPS) * nw_ref[:, hs] * (gv * _sigmoid(gv))).astype(o_ref.dtype)
        return carry

    lax.fori_loop(0, HGRN_HEADS, head, 0, unroll=4)

    @pl.when(cidx == pl.num_programs(1) - 1)
    def _():
        def fin(h, carry):
            sout_ref[h] = st_ref[h].T
            return carry
        lax.fori_loop(0, HGRN_HEADS, fin, 0)


def hgrn_branch(proj, row0, B, L, s0, log_lb, log1m_lb, norm_w, col0, precise):
    dk = HGRN_HEADS * HGRN_DK
    C = _pick_tile(L, (HGRN_CHUNK, 64, 32, 16))
    nc = L // C
    rb0 = row0 // C
    cb0 = col0 // dk
    tri = jnp.asarray(np.tril(np.ones((C, C), np.float32)), BF16)
    lvl = jnp.asarray(_pair_levels(C))
    kern = functools.partial(_hgrn_kernel, C=C, precise=precise)

    def pspec(j):
        return pl.BlockSpec((C, dk), functools.partial(lambda b, c, j: (rb0 + b * nc + c, cb0 + j), j=j))

    vec = pl.BlockSpec((1, dk), lambda b, c: (0, 0))
    sspec = pl.BlockSpec((None, HGRN_HEADS, HGRN_DK, HGRN_DV), lambda b, c: (b, 0, 0, 0))
    return pl.pallas_call(
        kern,
        out_shape=(jax.ShapeDtypeStruct((B * L, dk), BF16),
                   jax.ShapeDtypeStruct((B, HGRN_HEADS, HGRN_DK, HGRN_DV), F32)),
        grid=(B, nc),
        in_specs=[pspec(0), pspec(1), pspec(2), pspec(3), sspec, vec, vec, vec,
                  pl.BlockSpec((C, C), lambda b, c: (0, 0)), pl.BlockSpec((C, C), lambda b, c: (0, 0))],
        out_specs=(pl.BlockSpec((C, dk), lambda b, c: (b * nc + c, 0)), sspec),
        scratch_shapes=[pltpu.VMEM((HGRN_HEADS, HGRN_DV, HGRN_DK), F32), pltpu.VMEM((C, dk), F32),
                        pltpu.VMEM((C, dk), F32)],
        compiler_params=_cparams("arbitrary", "arbitrary"),
        name="hgrn_branch",
    )(proj, proj, proj, proj, s0, log_lb.reshape(1, dk), log1m_lb.reshape(1, dk), norm_w.reshape(1, dk), tri, lvl)


def _attn_kernel(q_ref, k_ref, v_ref, o_ref, *, scale):
    hd = q_ref.shape[1] // MEM_HEADS
    for h in range(MEM_HEADS):
        cs = slice(h * hd, (h + 1) * hd)
        s = _dot_nt(q_ref[:, cs].astype(BF16), k_ref[:, cs].astype(BF16)) * scale
        s = s - jnp.max(s, axis=-1, keepdims=True)
        e = jnp.exp(s)
        pr = e / jnp.sum(e, axis=-1, keepdims=True)
        o_ref[:, cs] = _dot(pr.astype(BF16), v_ref[:, cs].astype(BF16)).astype(o_ref.dtype)


def attn_branch(proj, row0, B, L, mem_k, mem_v, col0):
    n_mem, dm = mem_k.shape[1], mem_k.shape[2]
    tl = _pick_tile(L, (512, 256, 128, 64, 32, 16))
    nl = L // tl
    rb0 = row0 // tl
    cb0 = col0 // dm
    kern = functools.partial(_attn_kernel, scale=float((dm // MEM_HEADS) ** -0.5))
    mspec = pl.BlockSpec((None, n_mem, dm), lambda b, l: (b, 0, 0))
    return pl.pallas_call(
        kern,
        out_shape=jax.ShapeDtypeStruct((B * L, dm), BF16),
        grid=(B, nl),
        in_specs=[pl.BlockSpec((tl, dm), lambda b, l: (rb0 + b * nl + l, cb0)), mspec, mspec],
        out_specs=pl.BlockSpec((tl, dm), lambda b, l: (b * nl + l, 0)),
        compiler_params=_cparams("parallel", "parallel"),
        name="attn_branch",
    )(proj, mem_k, mem_v)


def _moe_kernel(be_ref, nr_ref, nv_ref, x_ref, win_ref, wout_ref, o_ref, acc_ref, hm_ref, *, n1, sb):
    b = pl.program_id(0)
    s = pl.program_id(1)
    nr = nr_ref[b]
    R = x_ref.shape[0]
    F, nc = wout_ref.shape
    nsub = (nr + sb - 1) // sb

    @pl.when(s < n1)
    def _():
        shift = ((s % 2) * 16).astype(jnp.uint32)
        for k in range(1, R // sb + 1):
            rows = slice(0, k * sb)

            @pl.when(nsub == k)
            def _():
                xk = lax.bitcast_convert_type((x_ref[rows, :] << shift) & jnp.uint32(0xFFFF0000), F32).astype(BF16)
                part = _dot(xk, win_ref[...].astype(BF16))

                @pl.when(s == 0)
                def _():
                    acc_ref[rows, :] = part

                @pl.when(s > 0)
                def _():
                    acc_ref[rows, :] += part

                @pl.when(s == n1 - 1)
                def _():
                    a = acc_ref[rows, :F]
                    hm_ref[rows, :] = (a * _sigmoid(a) * acc_ref[rows, F:]).astype(BF16)

    @pl.when(s >= n1)
    def _():
        for k in range(0, R // sb + 1):
            @pl.when(nsub == k)
            def _():
                if k > 0:
                    rows = slice(0, k * sb)
                    o_ref[rows, :] = _pack_pairs(_dot(hm_ref[rows, :], wout_ref[...].astype(BF16)))
                if k < R // sb:
                    o_ref[k * sb:, :] = jnp.zeros((R - k * sb, nc // 2), jnp.uint32)


def moe_experts(xs, blk_e, blk_rows, n_valid, w_ein, w_eout, layer):
    P = xs.shape[0]
    D, F2 = w_ein.shape[2], w_ein.shape[3]
    F = w_eout.shape[2]
    R = MOE_ROWS
    nb = P // R
    n1 = 4
    kc = D // n1
    nc = MOE_NC
    n2 = D // nc

    def x_chunk(b, s, be, nr, nv):
        return (jnp.minimum(b, nv[0] - 1), jnp.where(b < nv[0], jnp.minimum(s // 2, n1 // 2 - 1), n1 // 2 - 1))

    def in_chunk(b, s, be, nr, nv):
        sc = jnp.minimum(s, n1 - 1)
        return (layer, be[b], jnp.where(b < nv[0], (sc % 2) * (n1 // 2) + sc // 2, n1 - 1), 0)

    def out_chunk(b, s, be, nr, nv):
        early = s < 2
        e = jnp.where(early, be[jnp.maximum(b - 1, 0)], be[b])
        col = jnp.where(jnp.logical_or(early, b >= nv[0]), n2 - 1, jnp.maximum(s - n1, 0))
        return (layer, e, 0, col)

    def y_chunk(b, s, be, nr, nv):
        spare = b > nv[0]
        return (jnp.where(spare, nv[0], b), jnp.where(spare, n2 - 1, jnp.maximum(s - n1, 0)))

    return pl.pallas_call(
        functools.partial(_moe_kernel, n1=n1, sb=MOE_SUB),
        out_shape=jax.ShapeDtypeStruct((P, D // 2), jnp.uint32),
        grid_spec=pltpu.PrefetchScalarGridSpec(
            num_scalar_prefetch=3, grid=(nb, n1 + n2),
            in_specs=[pl.BlockSpec((R, kc), x_chunk),
                      pl.BlockSpec((None, None, kc, F2), in_chunk),
                      pl.BlockSpec((None, None, F, nc), out_chunk)],
            out_specs=pl.BlockSpec((R, nc // 2), y_chunk),
            scratch_shapes=[pltpu.VMEM((R, F2), F32), pltpu.VMEM((R, F), BF16)]),
        compiler_params=_cparams("arbitrary", "arbitrary"),
        name="moe_experts",
    )(blk_e, blk_rows, n_valid, xs, w_ein, w_eout)


def hier_moe(x, norm_w, w_rg, b_rg, w_re, b_re, w_ein, w_eout, layer, next_norm_w, n_first):
    T, D = x.shape
    n_pad = LANES - N_GROUPS - N_EXPERTS
    w_router = jnp.concatenate([w_rg, w_re, jnp.zeros((D, n_pad), F32)], axis=1)
    b_router = jnp.concatenate([b_rg, b_re, jnp.zeros((n_pad,), F32)]).reshape(1, LANES)
    h, route, lane_counts = rmsnorm_router(x, norm_w, w_router, b_router)
    eid = route[:, ROUTE_E0:ROUTE_E0 + TOP_K].astype(jnp.int32)
    rank = route[:, ROUTE_RANK0:ROUTE_RANK0 + TOP_K].astype(jnp.int32)
    gate = route[:, ROUTE_GATE0:ROUTE_GATE0 + TOP_K]
    counts = lane_counts[0, N_GROUPS:N_GROUPS + N_EXPERTS].astype(jnp.int32)
    R = MOE_ROWS
    A = T * TOP_K
    flat_e = eid.reshape(-1)
    padded = (counts + R - 1) // R * R
    pend = jnp.cumsum(padded)
    pstart = pend - padded
    dest = pstart[flat_e] + rank.reshape(-1)
    nb = -(-(A + N_EXPERTS * (R - 1)) // R)
    P = nb * R
    flat_tok = jnp.repeat(jnp.arange(T, dtype=jnp.int32), TOP_K)
    buf_tok = (jnp.arange(P, dtype=jnp.int32) % T).at[dest].set(flat_tok)
    blk = jnp.arange(nb, dtype=jnp.int32)
    n_valid = (pend[-1] // R).astype(jnp.int32)
    blk_raw = jnp.minimum(jnp.sum(blk[:, None] * R >= pend[None, :], axis=1), N_EXPERTS - 1).astype(jnp.int32)
    live = blk < n_valid
    blk_e = jnp.where(live, blk_raw, blk_raw[jnp.maximum(n_valid - 1, 0)])
    blk_rows = jnp.where(live, jnp.clip(counts[blk_raw] - (blk * R - pstart[blk_raw]), 0, R), 0).astype(jnp.int32)
    xs = jnp.take(h, buf_tok, axis=0, mode="clip")
    y = moe_experts(xs, blk_e, blk_rows, n_valid.reshape(1), w_ein, w_eout, layer)
    dest2 = dest.reshape(T, TOP_K)
    y0 = jnp.take(y, dest2[:, 0], axis=0, mode="clip")
    y1 = jnp.take(y, dest2[:, 1], axis=0, mode="clip")
    return moe_combine(x, y0, y1, gate, next_norm_w, MOE_NC, n_first)


def _layer(x, h, groups, log_lb, log1m_lb, p, layer, next_norm_w, n_first):
    dc = p['w_dw'].shape[1]
    dk = HGRN_HEADS * HGRN_DK
    dm = p['w_mem_out'].shape[1]
    D = x.shape[1]
    col_hgrn = 2 * dc
    col_mem = col_hgrn + 4 * dk
    col_gate = col_mem + dm
    proj, gates = dense([h], [p['w_in']], layer, F32, name="in_proj", tn=1024, tm_max=320, prob_from=col_gate)
    cs, os_, oms, conv_states, hgrn_states = [], [], [], [], []
    for row0, B, L, conv_state, s0, mem_k, mem_v in groups:
        c, ns = conv_branch(proj, row0, B, L, conv_state, p['w_dw'], p['b_dw'], p['conv_ln_g'], p['conv_ln_b'])
        o, s_new = hgrn_branch(proj, row0, B, L, s0, log_lb, log1m_lb, p['hgrn_norm'], col_hgrn,
                               precise=L <= HGRN_PRECISE_MAX_LEN)
        om = attn_branch(proj, row0, B, L, mem_k.reshape(B, -1, dm), mem_v.reshape(B, -1, dm), col_mem)
        cs.append(c); os_.append(o); oms.append(om); conv_states.append(ns); hgrn_states.append(s_new)
    cat = lambda xs: xs[0] if len(xs) == 1 else jnp.concatenate(xs, axis=0)
    merged = dense([cat(cs), cat(os_), cat(oms)], [p['w_conv_out'], p['w_hgrn_out'], p['w_mem_out']], layer, BF16,
                   gate_src=gates, gate_cols=[0, D, 2 * D], name="branch_merge")
    x = dense([merged], [p['w_out']], layer, F32, residual=x, name="out_proj", tn=1024, tm_max=320)
    outs = hier_moe(x, p['norm_ffn'], p['w_router_group'], p['b_router_group'], p['w_router_expert'],
                    p['b_router_expert'], p['w_exp_in'], p['w_exp_out'], layer, next_norm_w, n_first)
    return outs, conv_states, hgrn_states


def kernel(x_prompt, x_sample, state_conv, state_hgrn, cache_mem_k, cache_mem_v, mem_prompt, norm_mix, norm_mem, norm_ffn, norm_final, w_in, w_dw, b_dw, conv_ln_g, conv_ln_b, w_conv_out, hgrn_lb, hgrn_norm, w_hgrn_out, w_mem_kv, w_mem_out, w_out, w_router_group, b_router_group, w_router_expert, b_router_expert, w_exp_in, w_exp_out):
    depth = w_in.shape[0]
    Bp, Lp, D = x_prompt.shape
    Bs, Ls, _ = x_sample.shape
    n_mem = mem_prompt.shape[1]
    dc = w_dw.shape[2]
    dm = w_mem_out.shape[1]
    lb_all = jnp.cumsum(jax.nn.softmax(hgrn_lb.astype(F32), axis=0), axis=0)
    lb_all = lb_all - lb_all[:1]
    log_lb = jnp.log(lb_all)
    log1m_lb = jnp.log1p(-lb_all)
    x = jnp.concatenate([x_prompt.reshape(Bp * Lp, D), x_sample.reshape(Bs * Ls, D)], axis=0)
    h = rmsnorm(x, norm_mix[0], BF16)
    conv0 = jnp.zeros((Bp, CONV_WIDTH - 1, dc), F32)
    s0 = jnp.zeros((Bp, HGRN_HEADS, HGRN_DK, HGRN_DV), F32)
    mem_flat = mem_prompt.reshape(Bp * n_mem, D)
    conv_p, hgrn_p, mk_p, mv_p, conv_s, hgrn_s = [], [], [], [], [], []
    for l in range(depth):
        p = dict(norm_mix=norm_mix[l], w_in=w_in, w_dw=w_dw[l], b_dw=b_dw[l], conv_ln_g=conv_ln_g[l],
                 conv_ln_b=conv_ln_b[l], w_conv_out=w_conv_out, hgrn_norm=hgrn_norm[l], w_hgrn_out=w_hgrn_out,
                 w_mem_out=w_mem_out, w_out=w_out, norm_ffn=norm_ffn[l], w_router_group=w_router_group[l],
                 b_router_group=b_router_group[l], w_router_expert=w_router_expert[l],
                 b_router_expert=b_router_expert[l], w_exp_in=w_exp_in, w_exp_out=w_exp_out)
        kv = dense([rmsnorm(mem_flat, norm_mem[l], BF16)], [w_mem_kv], l, F32, name="mem_kv")
        mk = kv[:, :dm].reshape(Bp, n_mem, MEM_HEADS, dm // MEM_HEADS)
        mv = kv[:, dm:].reshape(Bp, n_mem, MEM_HEADS, dm // MEM_HEADS)
        groups = [(0, Bp, Lp, conv0, s0, mk, mv),
                  (Bp * Lp, Bs, Ls, state_conv[l], state_hgrn[l], cache_mem_k[l], cache_mem_v[l])]
        last = l == depth - 1
        outs, cstates, hstates = _layer(x, h, groups, log_lb[l], log1m_lb[l], p, l,
                                        norm_final if last else norm_mix[l + 1], Bp * Lp if last else None)
        if not last:
            x, h = outs
        conv_p.append(cstates[0]); hgrn_p.append(hstates[0]); mk_p.append(mk); mv_p.append(mv)
        conv_s.append(cstates[1]); hgrn_s.append(hstates[1])
    y_prompt = outs[0].reshape(Bp, Lp, D)
    y_sample = outs[1].reshape(Bs, Ls, D)
    return (y_prompt, y_sample, jnp.stack(conv_p), jnp.stack(hgrn_p), jnp.stack(mk_p), jnp.stack(mv_p),
            jnp.stack(conv_s), jnp.stack(hgrn_s))
```

```python
import functools

import numpy as np
import jax
import jax.numpy as jnp
from jax import lax
from jax.experimental import pallas as pl
from jax.experimental.pallas import tpu as pltpu

F32 = jnp.float32
BF16 = jnp.bfloat16

EPS = 1e-6
CONV_WIDTH = 31
HGRN_HEADS = 16
HGRN_DK = 128
HGRN_DV = 128
MEM_HEADS = 4
N_GROUPS = 4
EXPERTS_PER_GROUP = 8
N_EXPERTS = N_GROUPS * EXPERTS_PER_GROUP
TOP_K = 2

LANES = 128
SUBLANES = 8
VMEM_LIMIT_BYTES = 56 * 1024 * 1024
MM_TN = 512
MOE_ROWS = 640
MOE_NC = 1024
MOE_SUB = 128
CONV_PAD = 32
HGRN_CHUNK = 128
HGRN_PRECISE_MAX_LEN = 64


def _cparams(*sem):
    return pltpu.CompilerParams(dimension_semantics=sem, vmem_limit_bytes=VMEM_LIMIT_BYTES)


def _pick_tile(n, cands):
    for c in cands:
        if n % c == 0:
            return c
    return n


def _sigmoid(x):
    return 1.0 / (1.0 + jnp.exp(-x))


def _dot(a, b, precision=None):
    return jnp.dot(a, b, preferred_element_type=F32, precision=precision)


def _dot_nt(a, b, precision=None):
    return lax.dot_general(a, b, (((1,), (1,)), ((), ())), preferred_element_type=F32, precision=precision)


def _dot_tn(a, b, precision=None):
    return lax.dot_general(a, b, (((0,), (0,)), ((), ())), preferred_element_type=F32, precision=precision)


def _rms_body(x, g):
    ms = jnp.mean(x * x, axis=-1, keepdims=True)
    return x * lax.rsqrt(ms + EPS) * g


def _rms_kernel(x_ref, g_ref, o_ref):
    o_ref[...] = _rms_body(x_ref[...], g_ref[...]).astype(o_ref.dtype)


def rmsnorm(x, g, out_dtype):
    T, D = x.shape
    tm = _pick_tile(T, (320, 256, 128, 64, 32, 16, 8))
    return pl.pallas_call(
        _rms_kernel,
        out_shape=jax.ShapeDtypeStruct((T, D), out_dtype),
        grid=(T // tm,),
        in_specs=[pl.BlockSpec((tm, D), lambda i: (i, 0)), pl.BlockSpec((1, D), lambda i: (0, 0))],
        out_specs=pl.BlockSpec((tm, D), lambda i: (i, 0)),
        compiler_params=_cparams("parallel"),
        name="rmsnorm",
    )(x, g.reshape(1, D))


def _rms_concat_kernel(x1_ref, x2_ref, g_ref, x_ref, h_ref, *, n1_blocks):
    i = pl.program_id(0)

    def emit(src_ref):
        x = src_ref[...]
        x_ref[...] = x
        h_ref[...] = _rms_body(x, g_ref[...]).astype(h_ref.dtype)

    @pl.when(i < n1_blocks)
    def _():
        emit(x1_ref)

    @pl.when(i >= n1_blocks)
    def _():
        emit(x2_ref)


def rmsnorm_concat(x1, x2, g):
    (n1, D), n2 = x1.shape, x2.shape[0]
    tm = _pick_tile(int(np.gcd(n1, n2)), (128, 64, 32, 16, 8))
    nb1 = n1 // tm
    out = pl.BlockSpec((tm, D), lambda i: (i, 0))
    return pl.pallas_call(
        functools.partial(_rms_concat_kernel, n1_blocks=nb1),
        out_shape=(jax.ShapeDtypeStruct((n1 + n2, D), F32), jax.ShapeDtypeStruct((n1 + n2, D), BF16)),
        grid=((n1 + n2) // tm,),
        in_specs=[pl.BlockSpec((tm, D), lambda i: (jnp.minimum(i, nb1 - 1), 0)),
                  pl.BlockSpec((tm, D), lambda i: (jnp.maximum(i - nb1, 0), 0)),
                  pl.BlockSpec((1, D), lambda i: (0, 0))],
        out_specs=(out, out),
        compiler_params=_cparams("arbitrary"),
        name="rmsnorm_concat",
    )(x1, x2, g.reshape(1, D))


def _pack_pairs(h):
    half = h.shape[1] // 2
    bits = lax.bitcast_convert_type(h.astype(BF16).astype(F32), jnp.uint32)
    return bits[:, :half] | (bits[:, half:] >> 16)


def _unpack_pairs_f32(w):
    hi = lax.bitcast_convert_type(w & jnp.uint32(0xFFFF0000), F32)
    lo = lax.bitcast_convert_type(w << 16, F32)
    return hi, lo


ROUTE_E0, ROUTE_RANK0, ROUTE_GATE0 = 0, 2, 4


def _rms_router_kernel(x_ref, g_ref, wr_ref, br_ref, tri_ref, h_ref, rt_ref, cnt_ref, run_ref):
    @pl.when(pl.program_id(0) == 0)
    def _():
        run_ref[...] = jnp.zeros_like(run_ref)

    h = _rms_body(x_ref[...], g_ref[...])
    h_ref[...] = _pack_pairs(h)
    lg = jnp.dot(h, wr_ref[...], precision=lax.Precision.HIGHEST, preferred_element_type=F32) + br_ref[...]
    lane = lax.broadcasted_iota(jnp.int32, lg.shape, 1)
    lanef = lane.astype(F32)
    neg, far = -3.0e38, float(LANES)

    def first_max(v):
        m = jnp.max(v, axis=-1, keepdims=True)
        return m, jnp.min(jnp.where(v == m, lanef, far), axis=-1, keepdims=True)

    is_grp = lane < N_GROUPS
    gmax, grp = first_max(jnp.where(is_grp, lg, neg))
    p_grp = 1.0 / jnp.sum(jnp.where(is_grp, jnp.exp(lg - gmax), 0.0), axis=-1, keepdims=True)
    egrp = jnp.right_shift(lane - N_GROUPS, EXPERTS_PER_GROUP.bit_length() - 1).astype(F32)
    in_grp = (lane >= N_GROUPS) & (lane < N_GROUPS + N_EXPERTS) & (egrp == grp)
    el = jnp.where(in_grp, lg, neg)
    v1, i1 = first_max(el)
    v2, i2 = first_max(jnp.where(lanef == i1, neg, el))
    e2 = jnp.exp(v2 - v1)
    gate1 = p_grp * (1.0 / (1.0 + e2))
    gate2 = p_grp * (e2 / (1.0 + e2))
    hit1, hit2 = lanef == i1, lanef == i2
    chosen = jnp.where(hit1 | hit2, 1.0, 0.0)
    before = _dot(tri_ref[...], chosen.astype(BF16)) + run_ref[...]
    rank1 = jnp.sum(jnp.where(hit1, before, 0.0), axis=-1, keepdims=True)
    rank2 = jnp.sum(jnp.where(hit2, before, 0.0), axis=-1, keepdims=True)
    run_ref[...] += jnp.sum(chosen, axis=0, keepdims=True)
    cnt_ref[...] = run_ref[...]
    rt = jnp.where(lane == ROUTE_E0, i1 - N_GROUPS, jnp.where(lane == ROUTE_E0 + 1, i2 - N_GROUPS, 0.0))
    rt = jnp.where(lane == ROUTE_RANK0, rank1, jnp.where(lane == ROUTE_RANK0 + 1, rank2, rt))
    rt_ref[...] = jnp.where(lane == ROUTE_GATE0, gate1, jnp.where(lane == ROUTE_GATE0 + 1, gate2, rt))


def rmsnorm_router(x, g, w_router, b_router):
    T, D = x.shape
    tm = _pick_tile(T, (320, 256, 128, 64, 32, 16, 8))
    tri = jnp.asarray(np.tril(np.ones((tm, tm), np.float32), -1), BF16)
    full = lambda shape: pl.BlockSpec(shape, lambda i: (0, 0))
    return pl.pallas_call(
        _rms_router_kernel,
        out_shape=(jax.ShapeDtypeStruct((T, D // 2), jnp.uint32), jax.ShapeDtypeStruct((T, LANES), F32),
                   jax.ShapeDtypeStruct((1, LANES), F32)),
        grid=(T // tm,),
        in_specs=[pl.BlockSpec((tm, D), lambda i: (i, 0)), full((1, D)), full((D, LANES)), full((1, LANES)),
                  full((tm, tm))],
        out_specs=(pl.BlockSpec((tm, D // 2), lambda i: (i, 0)), pl.BlockSpec((tm, LANES), lambda i: (i, 0)),
                   full((1, LANES))),
        scratch_shapes=[pltpu.VMEM((1, LANES), F32)],
        compiler_params=_cparams("arbitrary"),
        name="rmsnorm_router",
    )(x, g.reshape(1, D), w_router, b_router, tri)


def _combine_kernel(x_ref, y0_ref, y1_ref, g_ref, nw_ref, o1_ref, o2_ref, xbuf, *, nc, n_first_blocks):
    D = x_ref.shape[1]
    g = g_ref[...]
    g0, g1 = g[:, 0:1], g[:, 1:2]
    ss = jnp.zeros((x_ref.shape[0], 1), F32)
    for n in range(D // nc):
        cw = slice(n * (nc // 2), (n + 1) * (nc // 2))
        hi0, lo0 = _unpack_pairs_f32(y0_ref[:, cw])
        hi1, lo1 = _unpack_pairs_f32(y1_ref[:, cw])
        ca = slice(n * nc, n * nc + nc // 2)
        cb = slice(n * nc + nc // 2, (n + 1) * nc)
        xa = x_ref[:, ca] + (hi0 * g0 + hi1 * g1)
        xb = x_ref[:, cb] + (lo0 * g0 + lo1 * g1)
        xbuf[:, ca] = xa
        xbuf[:, cb] = xb
        ss = ss + jnp.sum(xa * xa, axis=-1, keepdims=True) + jnp.sum(xb * xb, axis=-1, keepdims=True)
    xn = xbuf[...]
    y = xn * lax.rsqrt(ss * (1.0 / D) + EPS) * nw_ref[...]
    if n_first_blocks is None:
        o1_ref[...] = xn
        o2_ref[...] = y.astype(o2_ref.dtype)
    else:
        i = pl.program_id(0)

        @pl.when(i < n_first_blocks)
        def _():
            o1_ref[...] = y

        @pl.when(i >= n_first_blocks)
        def _():
            o2_ref[...] = y


def moe_combine(x, y0, y1, gate, norm_w, nc, n_first=None):
    T, D = x.shape
    tm = _pick_tile(T if n_first is None else int(np.gcd(n_first, T - n_first)), (128, 64, 32, 16, 8))
    spec = pl.BlockSpec((tm, D), lambda i: (i, 0))
    yspec = pl.BlockSpec((tm, D // 2), lambda i: (i, 0))
    if n_first is None:
        nb1 = None
        out_shape = (jax.ShapeDtypeStruct((T, D), F32), jax.ShapeDtypeStruct((T, D), BF16))
        out_specs = (spec, spec)
    else:
        nb1 = n_first // tm
        out_shape = (jax.ShapeDtypeStruct((n_first, D), F32), jax.ShapeDtypeStruct((T - n_first, D), F32))
        out_specs = (pl.BlockSpec((tm, D), lambda i: (jnp.minimum(i, nb1 - 1), 0)),
                     pl.BlockSpec((tm, D), lambda i: (jnp.maximum(i - nb1, 0), 0)))
    return pl.pallas_call(
        functools.partial(_combine_kernel, nc=nc, n_first_blocks=nb1), out_shape=out_shape, grid=(T // tm,),
        in_specs=[spec, yspec, yspec, pl.BlockSpec((tm, TOP_K), lambda i: (i, 0)),
                  pl.BlockSpec((1, D), lambda i: (0, 0))],
        out_specs=out_specs, scratch_shapes=[pltpu.VMEM((tm, D), F32)],
        compiler_params=_cparams("arbitrary"), name="moe_combine",
    )(x, y0, y1, gate, norm_w.reshape(1, D))


def _mm_kernel(*refs, n_terms, gated, residual):
    a_refs = refs[:n_terms]
    w_refs = refs[n_terms:2 * n_terms]
    pos = 2 * n_terms
    g_refs = refs[pos:pos + n_terms] if gated else ()
    pos += n_terms if gated else 0
    r_ref = refs[pos] if residual else None
    pos += 1 if residual else 0
    o_ref = refs[pos]
    wb_refs = refs[pos + 1:pos + 1 + n_terms]

    @pl.when(pl.program_id(1) == 0)
    def _():
        for w_ref, wb_ref in zip(w_refs, wb_refs):
            wb_ref[...] = w_ref[...].astype(BF16)

    acc = None
    for i in range(n_terms):
        y = _dot(a_refs[i][...], wb_refs[i][...])
        if gated:
            y = y * _sigmoid(g_refs[i][...])
        acc = y if acc is None else acc + y
    if residual:
        acc = acc + r_ref[...]
    o_ref[...] = acc.astype(o_ref.dtype)


def dense(a_list, w_list, layer, out_dtype, gate_src=None, gate_cols=None, residual=None, name="dense",
          tn=MM_TN, tm_max=640):
    n_terms = len(a_list)
    T = a_list[0].shape[0]
    N = w_list[0].shape[2]
    tm = _pick_tile(T, tuple(c for c in (640, 512, 320, 256, 128, 64, 32, 16, 8) if c <= tm_max))
    in_specs, args = [], []
    for a in a_list:
        in_specs.append(pl.BlockSpec((tm, a.shape[1]), lambda n, m: (m, 0)))
        args.append(a)
    for w in w_list:
        in_specs.append(pl.BlockSpec((None, w.shape[1], tn), lambda n, m: (layer, 0, n)))
        args.append(w)
    if gate_src is not None:
        for c0 in gate_cols:
            in_specs.append(pl.BlockSpec((tm, tn), functools.partial(lambda n, m, cb: (m, cb + n), cb=c0 // tn)))
            args.append(gate_src)
    if residual is not None:
        in_specs.append(pl.BlockSpec((tm, tn), lambda n, m: (m, n)))
        args.append(residual)
    kern = functools.partial(_mm_kernel, n_terms=n_terms, gated=gate_src is not None,
                             residual=residual is not None)
    return pl.pallas_call(
        kern,
        out_shape=jax.ShapeDtypeStruct((T, N), out_dtype),
        grid=(N // tn, T // tm),
        in_specs=in_specs,
        out_specs=pl.BlockSpec((tm, tn), lambda n, m: (m, n)),
        scratch_shapes=[pltpu.VMEM((w.shape[1], tn), BF16) for w in w_list],
        compiler_params=_cparams("arbitrary", "arbitrary"),
        name=name,
    )(*args)


def _fill_rows(call_kwargs, n_in, into):
    if into is None:
        return call_kwargs, (), lambda kern: kern
    kw = dict(call_kwargs, input_output_aliases={n_in: 0})
    kw['in_specs'] = list(kw['in_specs']) + [pl.BlockSpec(memory_space=pl.ANY)]

    def skip_alias(kern):
        def wrapped(*refs):
            return kern(*refs[:n_in], *refs[n_in + 1:])
        return wrapped
    return kw, (into,), skip_alias


def _conv_kernel(a_ref, b_ref, st_ref, w_ref, bdw_ref, lng_ref, lnb_ref, c_ref, ns_ref, ubuf, sh, cbuf, *, tl, rc):
    W1 = CONV_WIDTH - 1

    @pl.when(pl.program_id(1) == 0)
    def _():
        ubuf[0:W1, :] = st_ref[...]

    ubuf[W1:W1 + tl, :] = a_ref[...] * _sigmoid(b_ref[...])
    for r in range(1, SUBLANES):
        span = tl + SUBLANES * ((W1 - r) // SUBLANES)
        sh[r - 1, 0:span, :] = ubuf[r:r + span, :]
    dc = ubuf.shape[1]
    lc = LANES

    def chunk(i, carry):
        r0 = pl.multiple_of(i * rc, SUBLANES)
        for c0 in range(0, dc, lc):
            acc = jnp.zeros((rc, lc), F32)
            for j in range(CONV_WIDTH):
                q, r = divmod(j, SUBLANES)
                src = ubuf if r == 0 else sh.at[r - 1]
                acc = acc + src[pl.ds(r0 + SUBLANES * q, rc), c0:c0 + lc] * w_ref[j:j + 1, c0:c0 + lc]
            cbuf[pl.ds(r0, rc), c0:c0 + lc] = acc
        y = cbuf[pl.ds(r0, rc), :] + bdw_ref[...]
        mu = jnp.mean(y, axis=-1, keepdims=True)
        d = y - mu
        var = jnp.mean(d * d, axis=-1, keepdims=True)
        z = d * lax.rsqrt(var + EPS) * lng_ref[...] + lnb_ref[...]
        c_ref[pl.ds(r0, rc), :] = (z * _sigmoid(z)).astype(c_ref.dtype)
        return carry

    lax.fori_loop(0, tl // rc, chunk, 0)
    tail = ubuf[tl:tl + W1, :]
    ns_ref[...] = tail
    ubuf[0:W1, :] = tail


def conv_branch(proj, row0, B, L, state, w_dw, b_dw, ln_g, ln_b, into):
    dc = w_dw.shape[1]
    tl = _pick_tile(L, (512, 256, 128, 64, 32, 16))
    rc = min(tl, 64)
    nl = L // tl
    rb0 = row0 // tl
    kern = functools.partial(_conv_kernel, tl=tl, rc=rc)
    vec = pl.BlockSpec((1, dc), lambda b, l: (0, 0))
    rows = lambda b, l: (rb0 + b * nl + l, 0)
    kw, extra, adapt = _fill_rows(dict(
        out_shape=(jax.ShapeDtypeStruct((proj.shape[0], dc), BF16),
                   jax.ShapeDtypeStruct((B, CONV_WIDTH - 1, dc), F32)),
        grid=(B, nl),
        in_specs=[pl.BlockSpec((tl, dc), rows),
                  pl.BlockSpec((tl, dc), lambda b, l: (rb0 + b * nl + l, 1)),
                  pl.BlockSpec((None, CONV_WIDTH - 1, dc), lambda b, l: (b, 0, 0)),
                  pl.BlockSpec((CONV_WIDTH, dc), lambda b, l: (0, 0)), vec, vec, vec],
        out_specs=(pl.BlockSpec((tl, dc), rows),
                   pl.BlockSpec((None, CONV_WIDTH - 1, dc), lambda b, l: (b, 0, 0))),
        scratch_shapes=[pltpu.VMEM((tl + CONV_PAD, dc), F32),
                        pltpu.VMEM((SUBLANES - 1, tl + CONV_PAD - SUBLANES, dc), F32),
                        pltpu.VMEM((tl, dc), F32)],
        compiler_params=_cparams("arbitrary", "arbitrary"),
        name="conv_branch"), 7, into)
    return pl.pallas_call(adapt(kern), **kw)(
        proj, proj, state, w_dw, b_dw.reshape(1, dc), ln_g.reshape(1, dc), ln_b.reshape(1, dc), *extra)


def _pair_levels(C):
    t = np.arange(C)[:, None]
    s = np.arange(C)[None, :]
    x = np.bitwise_xor(t, s)
    lvl = np.where(x > 0, np.floor(np.log2(np.maximum(x, 1))).astype(np.int32) + 1, 0)
    return np.where(s <= t, lvl, -1).astype(np.int32)


def _group_boundary(b_ref, hs, m, C):
    if 2 * m >= SUBLANES:
        pieces = [jnp.broadcast_to(b_ref[pl.ds(g * 2 * m + m - 1, 1), hs], (2 * m, LANES))
                  for g in range(C // (2 * m))]
        return pieces[0] if len(pieces) == 1 else jnp.concatenate(pieces, axis=0)
    sub = lax.broadcasted_iota(jnp.int32, (SUBLANES, LANES), 0)
    pieces = []
    for v in range(C // SUBLANES):
        piece = jnp.broadcast_to(b_ref[pl.ds(v * SUBLANES + m - 1, 1), hs], (SUBLANES, LANES))
        for j in range(1, SUBLANES // (2 * m)):
            nxt = jnp.broadcast_to(b_ref[pl.ds(v * SUBLANES + j * 2 * m + m - 1, 1), hs], (SUBLANES, LANES))
            piece = jnp.where(sub >= j * 2 * m, nxt, piece)
        pieces.append(piece)
    return jnp.concatenate(pieces, axis=0)


def _hgrn_kernel(f_ref, i_ref, q_ref, g_ref, s0_ref, loglb_ref, log1m_ref, nw_ref, tri_ref, lvl_ref,
                 o_ref, sout_ref, st_ref, lf_ref, b_ref, *, C, precise):
    cidx = pl.program_id(1)
    prec = lax.Precision.HIGHEST if precise else None
    opd = (lambda a: a) if precise else (lambda a: a.astype(BF16))

    @pl.when(cidx == 0)
    def _():
        def init(h, carry):
            st_ref[h] = s0_ref[h].T
            return carry
        lax.fori_loop(0, HGRN_HEADS, init, 0)

    x = f_ref[...]
    log_sig = jnp.minimum(x, 0.0) - jnp.log(1.0 + jnp.exp(-jnp.abs(x)))
    la = loglb_ref[...]
    lc = log1m_ref[...] + log_sig
    lf = jnp.maximum(la, lc) + jnp.log(1.0 + jnp.exp(-jnp.abs(la - lc)))
    lf_ref[...] = lf
    hi = lf.astype(BF16)
    lo = (lf - hi.astype(F32)).astype(BF16)
    b_ref[...] = _dot(tri_ref[...], hi) + _dot(tri_ref[...], lo)

    row = lax.broadcasted_iota(jnp.int32, (C, LANES), 0)

    def head(h, carry):
        hs = pl.ds(pl.multiple_of(h * LANES, LANES), LANES)
        lf_h = lf_ref[:, hs]
        b = b_ref[:, hs]
        q = q_ref[:, hs]
        v = opd(i_ref[:, hs])
        k = 1.0 - jnp.exp(lf_h)
        lvl = lvl_ref[...]
        p = jnp.where(lvl == 0, _dot_nt(opd(q), opd(k), prec), 0.0)
        m, level = 1, 1
        while m < C:
            upper = (row & m) != 0
            if m == 1:
                xpo = jnp.where(upper, lf_h, 0.0)
            else:
                bnd = _group_boundary(b_ref, hs, m, C)
                xpo = jnp.where(upper, b - bnd, bnd - b)
            ex = jnp.exp(xpo)
            pm = _dot_nt(opd(q * ex), opd(k * ex), prec)
            p = jnp.where(lvl == level, pm, p)
            m *= 2
            level += 1
        s_t = st_ref[h]
        o = _dot(opd(p), v, prec) + _dot_nt(opd(q * jnp.exp(b)), opd(s_t), prec)
        b_last = b_ref[pl.ds(C - 1, 1), hs]
        k_dec = opd(k * jnp.exp(b_last - b))
        st_ref[h] = s_t * jnp.exp(b_last) + _dot_tn(v, k_dec, prec)
        ms = jnp.mean(o * o, axis=-1, keepdims=True)
        gv = g_ref[:, hs]
        o_ref[:, hs] = (o * lax.rsqrt(ms + EPS) * nw_ref[:, hs] * (gv * _sigmoid(gv))).astype(o_ref.dtype)
        return carry

    lax.fori_loop(0, HGRN_HEADS, head, 0, unroll=4)

    @pl.when(cidx == pl.num_programs(1) - 1)
    def _():
        def fin(h, carry):
            sout_ref[h] = st_ref[h].T
            return carry
        lax.fori_loop(0, HGRN_HEADS, fin, 0)


def hgrn_branch(proj, row0, B, L, s0, log_lb, log1m_lb, norm_w, col0, precise, into):
    dk = HGRN_HEADS * HGRN_DK
    C = _pick_tile(L, (HGRN_CHUNK, 64, 32, 16))
    nc = L // C
    rb0 = row0 // C
    cb0 = col0 // dk
    tri = jnp.asarray(np.tril(np.ones((C, C), np.float32)), BF16)
    lvl = jnp.asarray(_pair_levels(C))
    kern = functools.partial(_hgrn_kernel, C=C, precise=precise)

    def pspec(j):
        return pl.BlockSpec((C, dk), functools.partial(lambda b, c, j: (rb0 + b * nc + c, cb0 + j), j=j))

    vec = pl.BlockSpec((1, dk), lambda b, c: (0, 0))
    sspec = pl.BlockSpec((None, HGRN_HEADS, HGRN_DK, HGRN_DV), lambda b, c: (b, 0, 0, 0))
    kw, extra, adapt = _fill_rows(dict(
        out_shape=(jax.ShapeDtypeStruct((proj.shape[0], dk), BF16),
                   jax.ShapeDtypeStruct((B, HGRN_HEADS, HGRN_DK, HGRN_DV), F32)),
        grid=(B, nc),
        in_specs=[pspec(0), pspec(1), pspec(2), pspec(3), sspec, vec, vec, vec,
                  pl.BlockSpec((C, C), lambda b, c: (0, 0)), pl.BlockSpec((C, C), lambda b, c: (0, 0))],
        out_specs=(pl.BlockSpec((C, dk), lambda b, c: (rb0 + b * nc + c, 0)), sspec),
        scratch_shapes=[pltpu.VMEM((HGRN_HEADS, HGRN_DV, HGRN_DK), F32), pltpu.VMEM((C, dk), F32),
                        pltpu.VMEM((C, dk), F32)],
        compiler_params=_cparams("arbitrary", "arbitrary"),
        name="hgrn_branch"), 10, into)
    return pl.pallas_call(adapt(kern), **kw)(
        proj, proj, proj, proj, s0, log_lb.reshape(1, dk), log1m_lb.reshape(1, dk), norm_w.reshape(1, dk), tri, lvl,
        *extra)


def _attn_kernel(q_ref, k_ref, v_ref, o_ref, *, scale):
    hd = q_ref.shape[1] // MEM_HEADS
    for h in range(MEM_HEADS):
        cs = slice(h * hd, (h + 1) * hd)
        s = _dot_nt(q_ref[:, cs].astype(BF16), k_ref[:, cs].astype(BF16)) * scale
        s = s - jnp.max(s, axis=-1, keepdims=True)
        e = jnp.exp(s)
        pr = e / jnp.sum(e, axis=-1, keepdims=True)
        o_ref[:, cs] = _dot(pr.astype(BF16), v_ref[:, cs].astype(BF16)).astype(o_ref.dtype)


def attn_branch(proj, row0, B, L, mem_k, mem_v, col0, into):
    n_mem, dm = mem_k.shape[1], mem_k.shape[2]
    tl = _pick_tile(L, (512, 256, 128, 64, 32, 16))
    nl = L // tl
    rb0 = row0 // tl
    cb0 = col0 // dm
    kern = functools.partial(_attn_kernel, scale=float((dm // MEM_HEADS) ** -0.5))
    mspec = pl.BlockSpec((None, n_mem, dm), lambda b, l: (b, 0, 0))
    kw, extra, adapt = _fill_rows(dict(
        out_shape=jax.ShapeDtypeStruct((proj.shape[0], dm), BF16),
        grid=(B, nl),
        in_specs=[pl.BlockSpec((tl, dm), lambda b, l: (rb0 + b * nl + l, cb0)), mspec, mspec],
        out_specs=pl.BlockSpec((tl, dm), lambda b, l: (rb0 + b * nl + l, 0)),
        compiler_params=_cparams("parallel", "parallel"),
        name="attn_branch"), 3, into)
    return pl.pallas_call(adapt(kern), **kw)(proj, mem_k, mem_v, *extra)


def _moe_kernel(be_ref, nr_ref, nv_ref, x_ref, win_ref, wout_ref, o_ref, acc_ref, hm_ref, *, n1, sb):
    b = pl.program_id(0)
    s = pl.program_id(1)
    nr = nr_ref[b]
    R = x_ref.shape[0]
    F, nc = wout_ref.shape
    nsub = (nr + sb - 1) // sb

    @pl.when(s < n1)
    def _():
        shift = ((s % 2) * 16).astype(jnp.uint32)
        for k in range(1, R // sb + 1):
            rows = slice(0, k * sb)

            @pl.when(nsub == k)
            def _():
                xk = lax.bitcast_convert_type((x_ref[rows, :] << shift) & jnp.uint32(0xFFFF0000), F32).astype(BF16)
                part = _dot(xk, win_ref[...].astype(BF16))

                @pl.when(s == 0)
                def _():
                    acc_ref[rows, :] = part

                @pl.when(s > 0)
                def _():
                    acc_ref[rows, :] += part

                @pl.when(s == n1 - 1)
                def _():
                    a = acc_ref[rows, :F]
                    hm_ref[rows, :] = (a * _sigmoid(a) * acc_ref[rows, F:]).astype(BF16)

    @pl.when(s >= n1)
    def _():
        for k in range(0, R // sb + 1):
            @pl.when(nsub == k)
            def _():
                if k > 0:
                    rows = slice(0, k * sb)
                    o_ref[rows, :] = _pack_pairs(_dot(hm_ref[rows, :], wout_ref[...].astype(BF16)))
                if k < R // sb:
                    o_ref[k * sb:, :] = jnp.zeros((R - k * sb, nc // 2), jnp.uint32)


def moe_experts(xs, blk_e, blk_rows, n_valid, w_ein, w_eout, layer):
    P = xs.shape[0]
    D, F2 = w_ein.shape[2], w_ein.shape[3]
    F = w_eout.shape[2]
    R = MOE_ROWS
    nb = P // R
    n1 = 4
    kc = D // n1
    nc = MOE_NC
    n2 = D // nc

    def x_chunk(b, s, be, nr, nv):
        return (jnp.minimum(b, nv[0] - 1), jnp.where(b < nv[0], jnp.minimum(s // 2, n1 // 2 - 1), n1 // 2 - 1))

    def in_chunk(b, s, be, nr, nv):
        sc = jnp.minimum(s, n1 - 1)
        return (layer, be[b], jnp.where(b < nv[0], (sc % 2) * (n1 // 2) + sc // 2, n1 - 1), 0)

    def out_chunk(b, s, be, nr, nv):
        early = s < 2
        e = jnp.where(early, be[jnp.maximum(b - 1, 0)], be[b])
        col = jnp.where(jnp.logical_or(early, b >= nv[0]), n2 - 1, jnp.maximum(s - n1, 0))
        return (layer, e, 0, col)

    def y_chunk(b, s, be, nr, nv):
        return (b, jnp.maximum(s - n1, 0))

    return pl.pallas_call(
        functools.partial(_moe_kernel, n1=n1, sb=MOE_SUB),
        out_shape=jax.ShapeDtypeStruct((P, D // 2), jnp.uint32),
        grid_spec=pltpu.PrefetchScalarGridSpec(
            num_scalar_prefetch=3, grid=(nb, n1 + n2),
            in_specs=[pl.BlockSpec((R, kc), x_chunk),
                      pl.BlockSpec((None, None, kc, F2), in_chunk),
                      pl.BlockSpec((None, None, F, nc), out_chunk)],
            out_specs=pl.BlockSpec((R, nc // 2), y_chunk),
            scratch_shapes=[pltpu.VMEM((R, F2), F32), pltpu.VMEM((R, F), BF16)]),
        compiler_params=_cparams("arbitrary", "arbitrary"),
        name="moe_experts",
    )(blk_e, blk_rows, n_valid, xs, w_ein, w_eout)


def hier_moe(x, norm_w, w_rg, b_rg, w_re, b_re, w_ein, w_eout, layer, next_norm_w, n_first):
    T, D = x.shape
    n_pad = LANES - N_GROUPS - N_EXPERTS
    w_router = jnp.concatenate([w_rg, w_re, jnp.zeros((D, n_pad), F32)], axis=1)
    b_router = jnp.concatenate([b_rg, b_re, jnp.zeros((n_pad,), F32)]).reshape(1, LANES)
    h, route, lane_counts = rmsnorm_router(x, norm_w, w_router, b_router)
    eid = route[:, ROUTE_E0:ROUTE_E0 + TOP_K].astype(jnp.int32)
    rank = route[:, ROUTE_RANK0:ROUTE_RANK0 + TOP_K].astype(jnp.int32)
    gate = route[:, ROUTE_GATE0:ROUTE_GATE0 + TOP_K]
    counts = lane_counts[0, N_GROUPS:N_GROUPS + N_EXPERTS].astype(jnp.int32)
    R = MOE_ROWS
    A = T * TOP_K
    flat_e = eid.reshape(-1)
    padded = (counts + R - 1) // R * R
    pend = jnp.cumsum(padded)
    pstart = pend - padded
    dest = pstart[flat_e] + rank.reshape(-1)
    nb = -(-(A + N_EXPERTS * (R - 1)) // R)
    P = nb * R
    flat_tok = jnp.repeat(jnp.arange(T, dtype=jnp.int32), TOP_K)
    buf_tok = (jnp.arange(P, dtype=jnp.int32) % T).at[dest].set(flat_tok)
    blk = jnp.arange(nb, dtype=jnp.int32)
    n_valid = (pend[-1] // R).astype(jnp.int32)
    blk_raw = jnp.minimum(jnp.sum(blk[:, None] * R >= pend[None, :], axis=1), N_EXPERTS - 1).astype(jnp.int32)
    live = blk < n_valid
    blk_e = jnp.where(live, blk_raw, blk_raw[jnp.maximum(n_valid - 1, 0)])
    blk_rows = jnp.where(live, jnp.clip(counts[blk_raw] - (blk * R - pstart[blk_raw]), 0, R), 0).astype(jnp.int32)
    xs = jnp.take(h, buf_tok, axis=0, mode="clip")
    y = moe_experts(xs, blk_e, blk_rows, n_valid.reshape(1), w_ein, w_eout, layer)
    dest2 = dest.reshape(T, TOP_K)
    y0 = jnp.take(y, dest2[:, 0], axis=0, mode="clip")
    y1 = jnp.take(y, dest2[:, 1], axis=0, mode="clip")
    return moe_combine(x, y0, y1, gate, next_norm_w, MOE_NC, n_first)


def _layer(x, h, groups, log_lb, log1m_lb, p, layer, next_norm_w, n_first):
    dc = p['w_dw'].shape[1]
    dk = HGRN_HEADS * HGRN_DK
    dm = p['w_mem_out'].shape[1]
    D = x.shape[1]
    proj = dense([h], [p['w_in']], layer, F32, name="in_proj", tn=1024, tm_max=320)
    col_hgrn = 2 * dc
    col_mem = col_hgrn + 4 * dk
    col_gate = col_mem + dm
    c = o = om = None
    conv_states, hgrn_states = [], []
    for row0, B, L, conv_state, s0, mem_k, mem_v in groups:
        c, ns = conv_branch(proj, row0, B, L, conv_state, p['w_dw'], p['b_dw'], p['conv_ln_g'], p['conv_ln_b'], c)
        o, s_new = hgrn_branch(proj, row0, B, L, s0, log_lb, log1m_lb, p['hgrn_norm'], col_hgrn,
                               L <= HGRN_PRECISE_MAX_LEN, o)
        om = attn_branch(proj, row0, B, L, mem_k.reshape(B, -1, dm), mem_v.reshape(B, -1, dm), col_mem, om)
        conv_states.append(ns); hgrn_states.append(s_new)
    merged = dense([c, o, om], [p['w_conv_out'], p['w_hgrn_out'], p['w_mem_out']], layer, BF16,
                   gate_src=proj, gate_cols=[col_gate, col_gate + D, col_gate + 2 * D], name="branch_merge")
    x = dense([merged], [p['w_out']], layer, F32, residual=x, name="out_proj", tn=1024, tm_max=320)
    outs = hier_moe(x, p['norm_ffn'], p['w_router_group'], p['b_router_group'], p['w_router_expert'],
                    p['b_router_expert'], p['w_exp_in'], p['w_exp_out'], layer, next_norm_w, n_first)
    return outs, conv_states, hgrn_states


def kernel(x_prompt, x_sample, state_conv, state_hgrn, cache_mem_k, cache_mem_v, mem_prompt, norm_mix, norm_mem, norm_ffn, norm_final, w_in, w_dw, b_dw, conv_ln_g, conv_ln_b, w_conv_out, hgrn_lb, hgrn_norm, w_hgrn_out, w_mem_kv, w_mem_out, w_out, w_router_group, b_router_group, w_router_expert, b_router_expert, w_exp_in, w_exp_out):
    depth = w_in.shape[0]
    Bp, Lp, D = x_prompt.shape
    Bs, Ls, _ = x_sample.shape
    n_mem = mem_prompt.shape[1]
    dc = w_dw.shape[2]
    dm = w_mem_out.shape[1]
    lb_all = jnp.cumsum(jax.nn.softmax(hgrn_lb.astype(F32), axis=0), axis=0)
    lb_all = lb_all - lb_all[:1]
    log_lb = jnp.log(lb_all)
    log1m_lb = jnp.log1p(-lb_all)
    x, h = rmsnorm_concat(x_prompt.reshape(Bp * Lp, D), x_sample.reshape(Bs * Ls, D), norm_mix[0])
    conv0 = jnp.zeros((Bp, CONV_WIDTH - 1, dc), F32)
    s0 = jnp.zeros((Bp, HGRN_HEADS, HGRN_DK, HGRN_DV), F32)
    mem_flat = mem_prompt.reshape(Bp * n_mem, D)
    conv_p, hgrn_p, mk_p, mv_p, conv_s, hgrn_s = [], [], [], [], [], []
    for l in range(depth):
        p = dict(norm_mix=norm_mix[l], w_in=w_in, w_dw=w_dw[l], b_dw=b_dw[l], conv_ln_g=conv_ln_g[l],
                 conv_ln_b=conv_ln_b[l], w_conv_out=w_conv_out, hgrn_norm=hgrn_norm[l], w_hgrn_out=w_hgrn_out,
                 w_mem_out=w_mem_out, w_out=w_out, norm_ffn=norm_ffn[l], w_router_group=w_router_group[l],
                 b_router_group=b_router_group[l], w_router_expert=w_router_expert[l],
                 b_router_expert=b_router_expert[l], w_exp_in=w_exp_in, w_exp_out=w_exp_out)
        kv = dense([rmsnorm(mem_flat, norm_mem[l], BF16)], [w_mem_kv], l, F32, name="mem_kv")
        mk = kv[:, :dm].reshape(Bp, n_mem, MEM_HEADS, dm // MEM_HEADS)
        mv = kv[:, dm:].reshape(Bp, n_mem, MEM_HEADS, dm // MEM_HEADS)
        groups = [(0, Bp, Lp, conv0, s0, mk, mv),
                  (Bp * Lp, Bs, Ls, state_conv[l], state_hgrn[l], cache_mem_k[l], cache_mem_v[l])]
        last = l == depth - 1
        outs, cstates, hstates = _layer(x, h, groups, log_lb[l], log1m_lb[l], p, l,
                                        norm_final if last else norm_mix[l + 1], Bp * Lp if last else None)
        if not last:
            x, h = outs
        conv_p.append(cstates[0]); hgrn_p.append(hstates[0]); mk_p.append(mk); mv_p.append(mv)
        conv_s.append(cstates[1]); hgrn_s.append(hstates[1])
    y_prompt = outs[0].reshape(Bp, Lp, D)
    y_sample = outs[1].reshape(Bs, Ls, D)
    return (y_prompt, y_sample, jnp.stack(conv_p), jnp.stack(hgrn_p), jnp.stack(mk_p), jnp.stack(mv_p),
            jnp.stack(conv_s), jnp.stack(hgrn_s))
```

```python
import functools

import numpy as np
import jax
import jax.numpy as jnp
from jax import lax
from jax.experimental import pallas as pl
from jax.experimental.pallas import tpu as pltpu

F32 = jnp.float32
BF16 = jnp.bfloat16

EPS = 1e-6
CONV_WIDTH = 31
HGRN_HEADS = 16
HGRN_DK = 128
HGRN_DV = 128
MEM_HEADS = 4
N_GROUPS = 4
EXPERTS_PER_GROUP = 8
N_EXPERTS = N_GROUPS * EXPERTS_PER_GROUP
TOP_K = 2

LANES = 128
SUBLANES = 8
VMEM_LIMIT_BYTES = 56 * 1024 * 1024
MM_TN = 512
MOE_ROWS = 640
MOE_NC = 2048
MOE_SUB = 128
CONV_PAD = 32
HGRN_CHUNK = 128
HGRN_PRECISE_MAX_LEN = 64


def _cparams(*sem):
    return pltpu.CompilerParams(dimension_semantics=sem, vmem_limit_bytes=VMEM_LIMIT_BYTES)


def _pick_tile(n, cands):
    for c in cands:
        if n % c == 0:
            return c
    return n


def _sigmoid(x):
    return 1.0 / (1.0 + jnp.exp(-x))


def _dot(a, b, precision=None):
    return jnp.dot(a, b, preferred_element_type=F32, precision=precision)


def _dot_nt(a, b, precision=None):
    return lax.dot_general(a, b, (((1,), (1,)), ((), ())), preferred_element_type=F32, precision=precision)


def _dot_tn(a, b, precision=None):
    return lax.dot_general(a, b, (((0,), (0,)), ((), ())), preferred_element_type=F32, precision=precision)


def _rms_body(x, g):
    ms = jnp.mean(x * x, axis=-1, keepdims=True)
    return x * lax.rsqrt(ms + EPS) * g


def _rms_kernel(x_ref, g_ref, o_ref):
    o_ref[...] = _rms_body(x_ref[...], g_ref[...]).astype(o_ref.dtype)


def rmsnorm(x, g, out_dtype):
    T, D = x.shape
    tm = _pick_tile(T, (320, 256, 128, 64, 32, 16, 8))
    return pl.pallas_call(
        _rms_kernel,
        out_shape=jax.ShapeDtypeStruct((T, D), out_dtype),
        grid=(T // tm,),
        in_specs=[pl.BlockSpec((tm, D), lambda i: (i, 0)), pl.BlockSpec((1, D), lambda i: (0, 0))],
        out_specs=pl.BlockSpec((tm, D), lambda i: (i, 0)),
        compiler_params=_cparams("parallel"),
        name="rmsnorm",
    )(x, g.reshape(1, D))


def _rms_concat_kernel(x1_ref, x2_ref, g_ref, x_ref, h_ref, *, n1_blocks):
    i = pl.program_id(0)

    def emit(src_ref):
        x = src_ref[...]
        x_ref[...] = x
        h_ref[...] = _rms_body(x, g_ref[...]).astype(h_ref.dtype)

    @pl.when(i < n1_blocks)
    def _():
        emit(x1_ref)

    @pl.when(i >= n1_blocks)
    def _():
        emit(x2_ref)


def rmsnorm_concat(x1, x2, g):
    (n1, D), n2 = x1.shape, x2.shape[0]
    tm = _pick_tile(int(np.gcd(n1, n2)), (128, 64, 32, 16, 8))
    nb1 = n1 // tm
    out = pl.BlockSpec((tm, D), lambda i: (i, 0))
    return pl.pallas_call(
        functools.partial(_rms_concat_kernel, n1_blocks=nb1),
        out_shape=(jax.ShapeDtypeStruct((n1 + n2, D), F32), jax.ShapeDtypeStruct((n1 + n2, D), BF16)),
        grid=((n1 + n2) // tm,),
        in_specs=[pl.BlockSpec((tm, D), lambda i: (jnp.minimum(i, nb1 - 1), 0)),
                  pl.BlockSpec((tm, D), lambda i: (jnp.maximum(i - nb1, 0), 0)),
                  pl.BlockSpec((1, D), lambda i: (0, 0))],
        out_specs=(out, out),
        compiler_params=_cparams("arbitrary"),
        name="rmsnorm_concat",
    )(x1, x2, g.reshape(1, D))


def _pack_pairs(h):
    half = h.shape[1] // 2
    bits = lax.bitcast_convert_type(h.astype(BF16).astype(F32), jnp.uint32)
    return bits[:, :half] | (bits[:, half:] >> 16)


def _unpack_pairs_f32(w):
    hi = lax.bitcast_convert_type(w & jnp.uint32(0xFFFF0000), F32)
    lo = lax.bitcast_convert_type(w << 16, F32)
    return hi, lo


ROUTE_E0, ROUTE_RANK0, ROUTE_GATE0 = 0, 2, 4


def _rms_router_kernel(x_ref, g_ref, wr_ref, br_ref, tri_ref, h_ref, rt_ref, cnt_ref, run_ref):
    @pl.when(pl.program_id(0) == 0)
    def _():
        run_ref[...] = jnp.zeros_like(run_ref)

    h = _rms_body(x_ref[...], g_ref[...])
    h_ref[...] = _pack_pairs(h)
    lg = jnp.dot(h, wr_ref[...], precision=lax.Precision.HIGHEST, preferred_element_type=F32) + br_ref[...]
    lane = lax.broadcasted_iota(jnp.int32, lg.shape, 1)
    lanef = lane.astype(F32)
    neg, far = -3.0e38, float(LANES)

    def first_max(v):
        m = jnp.max(v, axis=-1, keepdims=True)
        return m, jnp.min(jnp.where(v == m, lanef, far), axis=-1, keepdims=True)

    is_grp = lane < N_GROUPS
    gmax, grp = first_max(jnp.where(is_grp, lg, neg))
    p_grp = 1.0 / jnp.sum(jnp.where(is_grp, jnp.exp(lg - gmax), 0.0), axis=-1, keepdims=True)
    egrp = jnp.right_shift(lane - N_GROUPS, EXPERTS_PER_GROUP.bit_length() - 1).astype(F32)
    in_grp = (lane >= N_GROUPS) & (lane < N_GROUPS + N_EXPERTS) & (egrp == grp)
    el = jnp.where(in_grp, lg, neg)
    v1, i1 = first_max(el)
    v2, i2 = first_max(jnp.where(lanef == i1, neg, el))
    e2 = jnp.exp(v2 - v1)
    gate1 = p_grp * (1.0 / (1.0 + e2))
    gate2 = p_grp * (e2 / (1.0 + e2))
    hit1, hit2 = lanef == i1, lanef == i2
    chosen = jnp.where(hit1 | hit2, 1.0, 0.0)
    before = _dot(tri_ref[...], chosen.astype(BF16)) + run_ref[...]
    rank1 = jnp.sum(jnp.where(hit1, before, 0.0), axis=-1, keepdims=True)
    rank2 = jnp.sum(jnp.where(hit2, before, 0.0), axis=-1, keepdims=True)
    run_ref[...] += jnp.sum(chosen, axis=0, keepdims=True)
    cnt_ref[...] = run_ref[...]
    rt = jnp.where(lane == ROUTE_E0, i1 - N_GROUPS, jnp.where(lane == ROUTE_E0 + 1, i2 - N_GROUPS, 0.0))
    rt = jnp.where(lane == ROUTE_RANK0, rank1, jnp.where(lane == ROUTE_RANK0 + 1, rank2, rt))
    rt_ref[...] = jnp.where(lane == ROUTE_GATE0, gate1, jnp.where(lane == ROUTE_GATE0 + 1, gate2, rt))


def rmsnorm_router(x, g, w_router, b_router):
    T, D = x.shape
    tm = _pick_tile(T, (320, 256, 128, 64, 32, 16, 8))
    tri = jnp.asarray(np.tril(np.ones((tm, tm), np.float32), -1), BF16)
    full = lambda shape: pl.BlockSpec(shape, lambda i: (0, 0))
    return pl.pallas_call(
        _rms_router_kernel,
        out_shape=(jax.ShapeDtypeStruct((T, D // 2), jnp.uint32), jax.ShapeDtypeStruct((T, LANES), F32),
                   jax.ShapeDtypeStruct((1, LANES), F32)),
        grid=(T // tm,),
        in_specs=[pl.BlockSpec((tm, D), lambda i: (i, 0)), full((1, D)), full((D, LANES)), full((1, LANES)),
                  full((tm, tm))],
        out_specs=(pl.BlockSpec((tm, D // 2), lambda i: (i, 0)), pl.BlockSpec((tm, LANES), lambda i: (i, 0)),
                   full((1, LANES))),
        scratch_shapes=[pltpu.VMEM((1, LANES), F32)],
        compiler_params=_cparams("arbitrary"),
        name="rmsnorm_router",
    )(x, g.reshape(1, D), w_router, b_router, tri)


def _combine_kernel(x_ref, y0_ref, y1_ref, g_ref, nw_ref, o1_ref, o2_ref, xbuf, *, nc, n_first_blocks):
    D = x_ref.shape[1]
    g = g_ref[...]
    g0, g1 = g[:, 0:1], g[:, 1:2]
    ss = jnp.zeros((x_ref.shape[0], 1), F32)
    for n in range(D // nc):
        cw = slice(n * (nc // 2), (n + 1) * (nc // 2))
        hi0, lo0 = _unpack_pairs_f32(y0_ref[:, cw])
        hi1, lo1 = _unpack_pairs_f32(y1_ref[:, cw])
        ca = slice(n * nc, n * nc + nc // 2)
        cb = slice(n * nc + nc // 2, (n + 1) * nc)
        xa = x_ref[:, ca] + (hi0 * g0 + hi1 * g1)
        xb = x_ref[:, cb] + (lo0 * g0 + lo1 * g1)
        xbuf[:, ca] = xa
        xbuf[:, cb] = xb
        ss = ss + jnp.sum(xa * xa, axis=-1, keepdims=True) + jnp.sum(xb * xb, axis=-1, keepdims=True)
    xn = xbuf[...]
    y = xn * lax.rsqrt(ss * (1.0 / D) + EPS) * nw_ref[...]
    if n_first_blocks is None:
        o1_ref[...] = xn
        o2_ref[...] = y.astype(o2_ref.dtype)
    else:
        i = pl.program_id(0)

        @pl.when(i < n_first_blocks)
        def _():
            o1_ref[...] = y

        @pl.when(i >= n_first_blocks)
        def _():
            o2_ref[...] = y


def moe_combine(x, y0, y1, gate, norm_w, nc, n_first=None):
    T, D = x.shape
    tm = _pick_tile(T if n_first is None else int(np.gcd(n_first, T - n_first)), (128, 64, 32, 16, 8))
    spec = pl.BlockSpec((tm, D), lambda i: (i, 0))
    yspec = pl.BlockSpec((tm, D // 2), lambda i: (i, 0))
    if n_first is None:
        nb1 = None
        out_shape = (jax.ShapeDtypeStruct((T, D), F32), jax.ShapeDtypeStruct((T, D), BF16))
        out_specs = (spec, spec)
    else:
        nb1 = n_first // tm
        out_shape = (jax.ShapeDtypeStruct((n_first, D), F32), jax.ShapeDtypeStruct((T - n_first, D), F32))
        out_specs = (pl.BlockSpec((tm, D), lambda i: (jnp.minimum(i, nb1 - 1), 0)),
                     pl.BlockSpec((tm, D), lambda i: (jnp.maximum(i - nb1, 0), 0)))
    return pl.pallas_call(
        functools.partial(_combine_kernel, nc=nc, n_first_blocks=nb1), out_shape=out_shape, grid=(T // tm,),
        in_specs=[spec, yspec, yspec, pl.BlockSpec((tm, TOP_K), lambda i: (i, 0)),
                  pl.BlockSpec((1, D), lambda i: (0, 0))],
        out_specs=out_specs, scratch_shapes=[pltpu.VMEM((tm, D), F32)],
        compiler_params=_cparams("arbitrary"), name="moe_combine",
    )(x, y0, y1, gate, norm_w.reshape(1, D))


def _mm_kernel(*refs, n_terms, gated, residual):
    a_refs = refs[:n_terms]
    w_refs = refs[n_terms:2 * n_terms]
    pos = 2 * n_terms
    g_refs = refs[pos:pos + n_terms] if gated else ()
    pos += n_terms if gated else 0
    r_ref = refs[pos] if residual else None
    pos += 1 if residual else 0
    o_ref = refs[pos]
    wb_refs = refs[pos + 1:pos + 1 + n_terms]

    @pl.when(pl.program_id(1) == 0)
    def _():
        for w_ref, wb_ref in zip(w_refs, wb_refs):
            wb_ref[...] = w_ref[...].astype(BF16)

    acc = None
    for i in range(n_terms):
        y = _dot(a_refs[i][...], wb_refs[i][...])
        if gated:
            y = y * _sigmoid(g_refs[i][...])
        acc = y if acc is None else acc + y
    if residual:
        acc = acc + r_ref[...]
    o_ref[...] = acc.astype(o_ref.dtype)


def dense(a_list, w_list, layer, out_dtype, gate_src=None, gate_cols=None, residual=None, name="dense",
          tn=MM_TN, tm_max=640):
    n_terms = len(a_list)
    T = a_list[0].shape[0]
    N = w_list[0].shape[2]
    tm = _pick_tile(T, tuple(c for c in (640, 512, 320, 256, 128, 64, 32, 16, 8) if c <= tm_max))
    in_specs, args = [], []
    for a in a_list:
        in_specs.append(pl.BlockSpec((tm, a.shape[1]), lambda n, m: (m, 0)))
        args.append(a)
    for w in w_list:
        in_specs.append(pl.BlockSpec((None, w.shape[1], tn), lambda n, m: (layer, 0, n)))
        args.append(w)
    if gate_src is not None:
        for c0 in gate_cols:
            in_specs.append(pl.BlockSpec((tm, tn), functools.partial(lambda n, m, cb: (m, cb + n), cb=c0 // tn)))
            args.append(gate_src)
    if residual is not None:
        in_specs.append(pl.BlockSpec((tm, tn), lambda n, m: (m, n)))
        args.append(residual)
    kern = functools.partial(_mm_kernel, n_terms=n_terms, gated=gate_src is not None,
                             residual=residual is not None)
    return pl.pallas_call(
        kern,
        out_shape=jax.ShapeDtypeStruct((T, N), out_dtype),
        grid=(N // tn, T // tm),
        in_specs=in_specs,
        out_specs=pl.BlockSpec((tm, tn), lambda n, m: (m, n)),
        scratch_shapes=[pltpu.VMEM((w.shape[1], tn), BF16) for w in w_list],
        compiler_params=_cparams("arbitrary", "arbitrary"),
        name=name,
    )(*args)


def _fill_rows(call_kwargs, n_in, into):
    if into is None:
        return call_kwargs, (), lambda kern: kern
    kw = dict(call_kwargs, input_output_aliases={n_in: 0})
    kw['in_specs'] = list(kw['in_specs']) + [pl.BlockSpec(memory_space=pl.ANY)]

    def skip_alias(kern):
        def wrapped(*refs):
            return kern(*refs[:n_in], *refs[n_in + 1:])
        return wrapped
    return kw, (into,), skip_alias


def _conv_kernel(a_ref, b_ref, st_ref, w_ref, bdw_ref, lng_ref, lnb_ref, c_ref, ns_ref, ubuf, sh, cbuf, *, tl, rc):
    W1 = CONV_WIDTH - 1

    @pl.when(pl.program_id(1) == 0)
    def _():
        ubuf[0:W1, :] = st_ref[...]

    ubuf[W1:W1 + tl, :] = a_ref[...] * _sigmoid(b_ref[...])
    for r in range(1, SUBLANES):
        span = tl + SUBLANES * ((W1 - r) // SUBLANES)
        sh[r - 1, 0:span, :] = ubuf[r:r + span, :]
    dc = ubuf.shape[1]
    lc = LANES

    def chunk(i, carry):
        r0 = pl.multiple_of(i * rc, SUBLANES)
        for c0 in range(0, dc, lc):
            acc = jnp.zeros((rc, lc), F32)
            for j in range(CONV_WIDTH):
                q, r = divmod(j, SUBLANES)
                src = ubuf if r == 0 else sh.at[r - 1]
                acc = acc + src[pl.ds(r0 + SUBLANES * q, rc), c0:c0 + lc] * w_ref[j:j + 1, c0:c0 + lc]
            cbuf[pl.ds(r0, rc), c0:c0 + lc] = acc
        y = cbuf[pl.ds(r0, rc), :] + bdw_ref[...]
        mu = jnp.mean(y, axis=-1, keepdims=True)
        d = y - mu
        var = jnp.mean(d * d, axis=-1, keepdims=True)
        z = d * lax.rsqrt(var + EPS) * lng_ref[...] + lnb_ref[...]
        c_ref[pl.ds(r0, rc), :] = (z * _sigmoid(z)).astype(c_ref.dtype)
        return carry

    lax.fori_loop(0, tl // rc, chunk, 0)
    tail = ubuf[tl:tl + W1, :]
    ns_ref[...] = tail
    ubuf[0:W1, :] = tail


def conv_branch(proj, row0, B, L, state, w_dw, b_dw, ln_g, ln_b, into):
    dc = w_dw.shape[1]
    tl = _pick_tile(L, (512, 256, 128, 64, 32, 16))
    rc = min(tl, 64)
    nl = L // tl
    rb0 = row0 // tl
    kern = functools.partial(_conv_kernel, tl=tl, rc=rc)
    vec = pl.BlockSpec((1, dc), lambda b, l: (0, 0))
    rows = lambda b, l: (rb0 + b * nl + l, 0)
    kw, extra, adapt = _fill_rows(dict(
        out_shape=(jax.ShapeDtypeStruct((proj.shape[0], dc), BF16),
                   jax.ShapeDtypeStruct((B, CONV_WIDTH - 1, dc), F32)),
        grid=(B, nl),
        in_specs=[pl.BlockSpec((tl, dc), rows),
                  pl.BlockSpec((tl, dc), lambda b, l: (rb0 + b * nl + l, 1)),
                  pl.BlockSpec((None, CONV_WIDTH - 1, dc), lambda b, l: (b, 0, 0)),
                  pl.BlockSpec((CONV_WIDTH, dc), lambda b, l: (0, 0)), vec, vec, vec],
        out_specs=(pl.BlockSpec((tl, dc), rows),
                   pl.BlockSpec((None, CONV_WIDTH - 1, dc), lambda b, l: (b, 0, 0))),
        scratch_shapes=[pltpu.VMEM((tl + CONV_PAD, dc), F32),
                        pltpu.VMEM((SUBLANES - 1, tl + CONV_PAD - SUBLANES, dc), F32),
                        pltpu.VMEM((tl, dc), F32)],
        compiler_params=_cparams("arbitrary", "arbitrary"),
        name="conv_branch"), 7, into)
    return pl.pallas_call(adapt(kern), **kw)(
        proj, proj, state, w_dw, b_dw.reshape(1, dc), ln_g.reshape(1, dc), ln_b.reshape(1, dc), *extra)


def _pair_levels(C):
    t = np.arange(C)[:, None]
    s = np.arange(C)[None, :]
    x = np.bitwise_xor(t, s)
    lvl = np.where(x > 0, np.floor(np.log2(np.maximum(x, 1))).astype(np.int32) + 1, 0)
    return np.where(s <= t, lvl, -1).astype(np.int32)


def _group_boundary(b_ref, hs, m, C):
    if 2 * m >= SUBLANES:
        pieces = [jnp.broadcast_to(b_ref[pl.ds(g * 2 * m + m - 1, 1), hs], (2 * m, LANES))
                  for g in range(C // (2 * m))]
        return pieces[0] if len(pieces) == 1 else jnp.concatenate(pieces, axis=0)
    sub = lax.broadcasted_iota(jnp.int32, (SUBLANES, LANES), 0)
    pieces = []
    for v in range(C // SUBLANES):
        piece = jnp.broadcast_to(b_ref[pl.ds(v * SUBLANES + m - 1, 1), hs], (SUBLANES, LANES))
        for j in range(1, SUBLANES // (2 * m)):
            nxt = jnp.broadcast_to(b_ref[pl.ds(v * SUBLANES + j * 2 * m + m - 1, 1), hs], (SUBLANES, LANES))
            piece = jnp.where(sub >= j * 2 * m, nxt, piece)
        pieces.append(piece)
    return jnp.concatenate(pieces, axis=0)


def _hgrn_kernel(f_ref, i_ref, q_ref, g_ref, s0_ref, loglb_ref, log1m_ref, nw_ref, tri_ref, lvl_ref,
                 o_ref, sout_ref, st_ref, lf_ref, b_ref, *, C, precise):
    cidx = pl.program_id(1)
    prec = lax.Precision.HIGHEST if precise else None
    opd = (lambda a: a) if precise else (lambda a: a.astype(BF16))

    @pl.when(cidx == 0)
    def _():
        def init(h, carry):
            st_ref[h] = s0_ref[h].T
            return carry
        lax.fori_loop(0, HGRN_HEADS, init, 0)

    x = f_ref[...]
    log_sig = jnp.minimum(x, 0.0) - jnp.log(1.0 + jnp.exp(-jnp.abs(x)))
    la = loglb_ref[...]
    lc = log1m_ref[...] + log_sig
    lf = jnp.maximum(la, lc) + jnp.log(1.0 + jnp.exp(-jnp.abs(la - lc)))
    lf_ref[...] = lf
    hi = lf.astype(BF16)
    lo = (lf - hi.astype(F32)).astype(BF16)
    b_ref[...] = _dot(tri_ref[...], hi) + _dot(tri_ref[...], lo)

    row = lax.broadcasted_iota(jnp.int32, (C, LANES), 0)

    def head(h, carry):
        hs = pl.ds(pl.multiple_of(h * LANES, LANES), LANES)
        lf_h = lf_ref[:, hs]
        b = b_ref[:, hs]
        q = q_ref[:, hs]
        v = opd(i_ref[:, hs])
        k = 1.0 - jnp.exp(lf_h)
        lvl = lvl_ref[...]
        p = jnp.where(lvl == 0, _dot_nt(opd(q), opd(k), prec), 0.0)
        m, level = 1, 1
        while m < C:
            upper = (row & m) != 0
            if m == 1:
                xpo = jnp.where(upper, lf_h, 0.0)
            else:
                bnd = _group_boundary(b_ref, hs, m, C)
                xpo = jnp.where(upper, b - bnd, bnd - b)
            ex = jnp.exp(xpo)
            pm = _dot_nt(opd(q * ex), opd(k * ex), prec)
            p = jnp.where(lvl == level, pm, p)
            m *= 2
            level += 1
        s_t = st_ref[h]
        o = _dot(opd(p), v, prec) + _dot_nt(opd(q * jnp.exp(b)), opd(s_t), prec)
        b_last = b_ref[pl.ds(C - 1, 1), hs]
        k_dec = opd(k * jnp.exp(b_last - b))
        st_ref[h] = s_t * jnp.exp(b_last) + _dot_tn(v, k_dec, prec)
        ms = jnp.mean(o * o, axis=-1, keepdims=True)
        gv = g_ref[:, hs]
        o_ref[:, hs] = (o * lax.rsqrt(ms + EPS) * nw_ref[:, hs] * (gv * _sigmoid(gv))).astype(o_ref.dtype)
        return carry

    lax.fori_loop(0, HGRN_HEADS, head, 0, unroll=4)

    @pl.when(cidx == pl.num_programs(1) - 1)
    def _():
        def fin(h, carry):
            sout_ref[h] = st_ref[h].T
            return carry
        lax.fori_loop(0, HGRN_HEADS, fin, 0)


def hgrn_branch(proj, row0, B, L, s0, log_lb, log1m_lb, norm_w, col0, precise, into):
    dk = HGRN_HEADS * HGRN_DK
    C = _pick_tile(L, (HGRN_CHUNK, 64, 32, 16))
    nc = L // C
    rb0 = row0 // C
    cb0 = col0 // dk
    tri = jnp.asarray(np.tril(np.ones((C, C), np.float32)), BF16)
    lvl = jnp.asarray(_pair_levels(C))
    kern = functools.partial(_hgrn_kernel, C=C, precise=precise)

    def pspec(j):
        return pl.BlockSpec((C, dk), functools.partial(lambda b, c, j: (rb0 + b * nc + c, cb0 + j), j=j))

    vec = pl.BlockSpec((1, dk), lambda b, c: (0, 0))
    sspec = pl.BlockSpec((None, HGRN_HEADS, HGRN_DK, HGRN_DV), lambda b, c: (b, 0, 0, 0))
    kw, extra, adapt = _fill_rows(dict(
        out_shape=(jax.ShapeDtypeStruct((proj.shape[0], dk), BF16),
                   jax.ShapeDtypeStruct((B, HGRN_HEADS, HGRN_DK, HGRN_DV), F32)),
        grid=(B, nc),
        in_specs=[pspec(0), pspec(1), pspec(2), pspec(3), sspec, vec, vec, vec,
                  pl.BlockSpec((C, C), lambda b, c: (0, 0)), pl.BlockSpec((C, C), lambda b, c: (0, 0))],
        out_specs=(pl.BlockSpec((C, dk), lambda b, c: (rb0 + b * nc + c, 0)), sspec),
        scratch_shapes=[pltpu.VMEM((HGRN_HEADS, HGRN_DV, HGRN_DK), F32), pltpu.VMEM((C, dk), F32),
                        pltpu.VMEM((C, dk), F32)],
        compiler_params=_cparams("arbitrary", "arbitrary"),
        name="hgrn_branch"), 10, into)
    return pl.pallas_call(adapt(kern), **kw)(
        proj, proj, proj, proj, s0, log_lb.reshape(1, dk), log1m_lb.reshape(1, dk), norm_w.reshape(1, dk), tri, lvl,
        *extra)


def _attn_kernel(q_ref, k_ref, v_ref, o_ref, *, scale):
    hd = q_ref.shape[1] // MEM_HEADS
    for h in range(MEM_HEADS):
        cs = slice(h * hd, (h + 1) * hd)
        s = _dot_nt(q_ref[:, cs].astype(BF16), k_ref[:, cs].astype(BF16)) * scale
        s = s - jnp.max(s, axis=-1, keepdims=True)
        e = jnp.exp(s)
        pr = e / jnp.sum(e, axis=-1, keepdims=True)
        o_ref[:, cs] = _dot(pr.astype(BF16), v_ref[:, cs].astype(BF16)).astype(o_ref.dtype)


def attn_branch(proj, row0, B, L, mem_k, mem_v, col0, into):
    n_mem, dm = mem_k.shape[1], mem_k.shape[2]
    tl = _pick_tile(L, (512, 256, 128, 64, 32, 16))
    nl = L // tl
    rb0 = row0 // tl
    cb0 = col0 // dm
    kern = functools.partial(_attn_kernel, scale=float((dm // MEM_HEADS) ** -0.5))
    mspec = pl.BlockSpec((None, n_mem, dm), lambda b, l: (b, 0, 0))
    kw, extra, adapt = _fill_rows(dict(
        out_shape=jax.ShapeDtypeStruct((proj.shape[0], dm), BF16),
        grid=(B, nl),
        in_specs=[pl.BlockSpec((tl, dm), lambda b, l: (rb0 + b * nl + l, cb0)), mspec, mspec],
        out_specs=pl.BlockSpec((tl, dm), lambda b, l: (rb0 + b * nl + l, 0)),
        compiler_params=_cparams("parallel", "parallel"),
        name="attn_branch"), 3, into)
    return pl.pallas_call(adapt(kern), **kw)(proj, mem_k, mem_v, *extra)


def _moe_kernel(be_ref, nr_ref, nv_ref, x_ref, win_ref, wout_ref, o_ref, acc_ref, hm_ref, *, n1, sb):
    b = pl.program_id(0)
    s = pl.program_id(1)
    nr = nr_ref[b]
    R = x_ref.shape[0]
    F, nc = wout_ref.shape
    nsub = (nr + sb - 1) // sb

    @pl.when(s < n1)
    def _():
        shift = ((s % 2) * 16).astype(jnp.uint32)
        for k in range(1, R // sb + 1):
            rows = slice(0, k * sb)

            @pl.when(nsub == k)
            def _():
                xk = lax.bitcast_convert_type((x_ref[rows, :] << shift) & jnp.uint32(0xFFFF0000), F32).astype(BF16)
                part = _dot(xk, win_ref[...].astype(BF16))

                @pl.when(s == 0)
                def _():
                    acc_ref[rows, :] = part

                @pl.when(s > 0)
                def _():
                    acc_ref[rows, :] += part

                @pl.when(s == n1 - 1)
                def _():
                    a = acc_ref[rows, :F]
                    hm_ref[rows, :] = (a * _sigmoid(a) * acc_ref[rows, F:]).astype(BF16)

    @pl.when(s >= n1)
    def _():
        for k in range(0, R // sb + 1):
            @pl.when(nsub == k)
            def _():
                if k > 0:
                    rows = slice(0, k * sb)
                    o_ref[rows, :] = _pack_pairs(_dot(hm_ref[rows, :], wout_ref[...].astype(BF16)))
                if k < R // sb:
                    o_ref[k * sb:, :] = jnp.zeros((R - k * sb, nc // 2), jnp.uint32)


def moe_experts(xs, blk_e, blk_rows, n_valid, w_ein, w_eout, layer):
    P = xs.shape[0]
    D, F2 = w_ein.shape[2], w_ein.shape[3]
    F = w_eout.shape[2]
    R = MOE_ROWS
    nb = P // R
    n1 = 4
    kc = D // n1
    nc = MOE_NC
    n2 = D // nc

    def x_chunk(b, s, be, nr, nv):
        return (jnp.minimum(b, nv[0] - 1), jnp.where(b < nv[0], jnp.minimum(s // 2, n1 // 2 - 1), n1 // 2 - 1))

    def in_chunk(b, s, be, nr, nv):
        sc = jnp.minimum(s, n1 - 1)
        return (layer, be[b], jnp.where(b < nv[0], (sc % 2) * (n1 // 2) + sc // 2, n1 - 1), 0)

    def out_chunk(b, s, be, nr, nv):
        early = s < 2
        e = jnp.where(early, be[jnp.maximum(b - 1, 0)], be[b])
        col = jnp.where(jnp.logical_or(early, b >= nv[0]), n2 - 1, jnp.maximum(s - n1, 0))
        return (layer, e, 0, col)

    def y_chunk(b, s, be, nr, nv):
        return (b, jnp.maximum(s - n1, 0))

    return pl.pallas_call(
        functools.partial(_moe_kernel, n1=n1, sb=MOE_SUB),
        out_shape=jax.ShapeDtypeStruct((P, D // 2), jnp.uint32),
        grid_spec=pltpu.PrefetchScalarGridSpec(
            num_scalar_prefetch=3, grid=(nb, n1 + n2),
            in_specs=[pl.BlockSpec((R, kc), x_chunk),
                      pl.BlockSpec((None, None, kc, F2), in_chunk),
                      pl.BlockSpec((None, None, F, nc), out_chunk)],
            out_specs=pl.BlockSpec((R, nc // 2), y_chunk),
            scratch_shapes=[pltpu.VMEM((R, F2), F32), pltpu.VMEM((R, F), BF16)]),
        compiler_params=_cparams("arbitrary", "arbitrary"),
        name="moe_experts",
    )(blk_e, blk_rows, n_valid, xs, w_ein, w_eout)


def hier_moe(x, norm_w, w_rg, b_rg, w_re, b_re, w_ein, w_eout, layer, next_norm_w, n_first):
    T, D = x.shape
    n_pad = LANES - N_GROUPS - N_EXPERTS
    w_router = jnp.concatenate([w_rg, w_re, jnp.zeros((D, n_pad), F32)], axis=1)
    b_router = jnp.concatenate([b_rg, b_re, jnp.zeros((n_pad,), F32)]).reshape(1, LANES)
    h, route, lane_counts = rmsnorm_router(x, norm_w, w_router, b_router)
    eid = route[:, ROUTE_E0:ROUTE_E0 + TOP_K].astype(jnp.int32)
    rank = route[:, ROUTE_RANK0:ROUTE_RANK0 + TOP_K].astype(jnp.int32)
    gate = route[:, ROUTE_GATE0:ROUTE_GATE0 + TOP_K]
    counts = lane_counts[0, N_GROUPS:N_GROUPS + N_EXPERTS].astype(jnp.int32)
    R = MOE_ROWS
    A = T * TOP_K
    flat_e = eid.reshape(-1)
    padded = (counts + R - 1) // R * R
    pend = jnp.cumsum(padded)
    pstart = pend - padded
    dest = pstart[flat_e] + rank.reshape(-1)
    nb = -(-(A + N_EXPERTS * (R - 1)) // R)
    P = nb * R
    flat_tok = jnp.repeat(jnp.arange(T, dtype=jnp.int32), TOP_K)
    buf_tok = (jnp.arange(P, dtype=jnp.int32) % T).at[dest].set(flat_tok)
    blk = jnp.arange(nb, dtype=jnp.int32)
    n_valid = (pend[-1] // R).astype(jnp.int32)
    blk_raw = jnp.minimum(jnp.sum(blk[:, None] * R >= pend[None, :], axis=1), N_EXPERTS - 1).astype(jnp.int32)
    live = blk < n_valid
    blk_e = jnp.where(live, blk_raw, blk_raw[jnp.maximum(n_valid - 1, 0)])
    blk_rows = jnp.where(live, jnp.clip(counts[blk_raw] - (blk * R - pstart[blk_raw]), 0, R), 0).astype(jnp.int32)
    xs = jnp.take(h, buf_tok, axis=0, mode="clip")
    y = moe_experts(xs, blk_e, blk_rows, n_valid.reshape(1), w_ein, w_eout, layer)
    dest2 = dest.reshape(T, TOP_K)
    y0 = jnp.take(y, dest2[:, 0], axis=0, mode="clip")
    y1 = jnp.take(y, dest2[:, 1], axis=0, mode="clip")
    return moe_combine(x, y0, y1, gate, next_norm_w, MOE_NC, n_first)


def _layer(x, h, groups, log_lb, log1m_lb, p, layer, next_norm_w, n_first):
    dc = p['w_dw'].shape[1]
    dk = HGRN_HEADS * HGRN_DK
    dm = p['w_mem_out'].shape[1]
    D = x.shape[1]
    proj = dense([h], [p['w_in']], layer, F32, name="in_proj", tn=1024, tm_max=320)
    col_hgrn = 2 * dc
    col_mem = col_hgrn + 4 * dk
    col_gate = col_mem + dm
    c = o = om = None
    conv_states, hgrn_states = [], []
    for row0, B, L, conv_state, s0, mem_k, mem_v in groups:
        c, ns = conv_branch(proj, row0, B, L, conv_state, p['w_dw'], p['b_dw'], p['conv_ln_g'], p['conv_ln_b'], c)
        o, s_new = hgrn_branch(proj, row0, B, L, s0, log_lb, log1m_lb, p['hgrn_norm'], col_hgrn,
                               L <= HGRN_PRECISE_MAX_LEN, o)
        om = attn_branch(proj, row0, B, L, mem_k.reshape(B, -1, dm), mem_v.reshape(B, -1, dm), col_mem, om)
        conv_states.append(ns); hgrn_states.append(s_new)
    merged = dense([c, o, om], [p['w_conv_out'], p['w_hgrn_out'], p['w_mem_out']], layer, BF16,
                   gate_src=proj, gate_cols=[col_gate, col_gate + D, col_gate + 2 * D], name="branch_merge")
    x = dense([merged], [p['w_out']], layer, F32, residual=x, name="out_proj", tn=1024, tm_max=320)
    outs = hier_moe(x, p['norm_ffn'], p['w_router_group'], p['b_router_group'], p['w_router_expert'],
                    p['b_router_expert'], p['w_exp_in'], p['w_exp_out'], layer, next_norm_w, n_first)
    return outs, conv_states, hgrn_states


def kernel(x_prompt, x_sample, state_conv, state_hgrn, cache_mem_k, cache_mem_v, mem_prompt, norm_mix, norm_mem, norm_ffn, norm_final, w_in, w_dw, b_dw, conv_ln_g, conv_ln_b, w_conv_out, hgrn_lb, hgrn_norm, w_hgrn_out, w_mem_kv, w_mem_out, w_out, w_router_group, b_router_group, w_router_expert, b_router_expert, w_exp_in, w_exp_out):
    depth = w_in.shape[0]
    Bp, Lp, D = x_prompt.shape
    Bs, Ls, _ = x_sample.shape
    n_mem = mem_prompt.shape[1]
    dc = w_dw.shape[2]
    dm = w_mem_out.shape[1]
    lb_all = jnp.cumsum(jax.nn.softmax(hgrn_lb.astype(F32), axis=0), axis=0)
    lb_all = lb_all - lb_all[:1]
    log_lb = jnp.log(lb_all)
    log1m_lb = jnp.log1p(-lb_all)
    x, h = rmsnorm_concat(x_prompt.reshape(Bp * Lp, D), x_sample.reshape(Bs * Ls, D), norm_mix[0])
    conv0 = jnp.zeros((Bp, CONV_WIDTH - 1, dc), F32)
    s0 = jnp.zeros((Bp, HGRN_HEADS, HGRN_DK, HGRN_DV), F32)
    mem_flat = mem_prompt.reshape(Bp * n_mem, D)
    conv_p, hgrn_p, mk_p, mv_p, conv_s, hgrn_s = [], [], [], [], [], []
    for l in range(depth):
        p = dict(norm_mix=norm_mix[l], w_in=w_in, w_dw=w_dw[l], b_dw=b_dw[l], conv_ln_g=conv_ln_g[l],
                 conv_ln_b=conv_ln_b[l], w_conv_out=w_conv_out, hgrn_norm=hgrn_norm[l], w_hgrn_out=w_hgrn_out,
                 w_mem_out=w_mem_out, w_out=w_out, norm_ffn=norm_ffn[l], w_router_group=w_router_group[l],
                 b_router_group=b_router_group[l], w_router_expert=w_router_expert[l],
                 b_router_expert=b_router_expert[l], w_exp_in=w_exp_in, w_exp_out=w_exp_out)
        kv = dense([rmsnorm(mem_flat, norm_mem[l], BF16)], [w_mem_kv], l, F32, name="mem_kv")
        mk = kv[:, :dm].reshape(Bp, n_mem, MEM_HEADS, dm // MEM_HEADS)
        mv = kv[:, dm:].reshape(Bp, n_mem, MEM_HEADS, dm // MEM_HEADS)
        groups = [(0, Bp, Lp, conv0, s0, mk, mv),
                  (Bp * Lp, Bs, Ls, state_conv[l], state_hgrn[l], cache_mem_k[l], cache_mem_v[l])]
        last = l == depth - 1
        outs, cstates, hstates = _layer(x, h, groups, log_lb[l], log1m_lb[l], p, l,
                                        norm_final if last else norm_mix[l + 1], Bp * Lp if last else None)
        if not last:
            x, h = outs
        conv_p.append(cstates[0]); hgrn_p.append(hstates[0]); mk_p.append(mk); mv_p.append(mv)
        conv_s.append(cstates[1]); hgrn_s.append(hstates[1])
    y_prompt = outs[0].reshape(Bp, Lp, D)
    y_sample = outs[1].reshape(Bs, Ls, D)
    return (y_prompt, y_sample, jnp.stack(conv_p), jnp.stack(hgrn_p), jnp.stack(mk_p), jnp.stack(mv_p),
            jnp.stack(conv_s), jnp.stack(hgrn_s))
```

```python
import functools

import numpy as np
import jax
import jax.numpy as jnp
from jax import lax
from jax.experimental import pallas as pl
from jax.experimental.pallas import tpu as pltpu

F32 = jnp.float32
BF16 = jnp.bfloat16

EPS = 1e-6
CONV_WIDTH = 31
HGRN_HEADS = 16
HGRN_DK = 128
HGRN_DV = 128
MEM_HEADS = 4
N_GROUPS = 4
EXPERTS_PER_GROUP = 8
N_EXPERTS = N_GROUPS * EXPERTS_PER_GROUP
TOP_K = 2

LANES = 128
SUBLANES = 8
VMEM_LIMIT_BYTES = 56 * 1024 * 1024
MM_TN = 512
MOE_ROWS = 640
MOE_NC = 2048
MOE_SUB = 128
CONV_PAD = 32
HGRN_CHUNK = 128
HGRN_PRECISE_MAX_LEN = 64


def _cparams(*sem):
    return pltpu.CompilerParams(dimension_semantics=sem, vmem_limit_bytes=VMEM_LIMIT_BYTES)


def _pick_tile(n, cands):
    for c in cands:
        if n % c == 0:
            return c
    return n


def _sigmoid(x):
    return 1.0 / (1.0 + jnp.exp(-x))


def _dot(a, b, precision=None):
    return jnp.dot(a, b, preferred_element_type=F32, precision=precision)


def _dot_nt(a, b, precision=None):
    return lax.dot_general(a, b, (((1,), (1,)), ((), ())), preferred_element_type=F32, precision=precision)


def _dot_tn(a, b, precision=None):
    return lax.dot_general(a, b, (((0,), (0,)), ((), ())), preferred_element_type=F32, precision=precision)


def _rms_body(x, g):
    ms = jnp.mean(x * x, axis=-1, keepdims=True)
    return x * lax.rsqrt(ms + EPS) * g


def _rms_kernel(x_ref, g_ref, o_ref):
    o_ref[...] = _rms_body(x_ref[...], g_ref[...]).astype(o_ref.dtype)


def rmsnorm(x, g, out_dtype):
    T, D = x.shape
    tm = _pick_tile(T, (320, 256, 128, 64, 32, 16, 8))
    return pl.pallas_call(
        _rms_kernel,
        out_shape=jax.ShapeDtypeStruct((T, D), out_dtype),
        grid=(T // tm,),
        in_specs=[pl.BlockSpec((tm, D), lambda i: (i, 0)), pl.BlockSpec((1, D), lambda i: (0, 0))],
        out_specs=pl.BlockSpec((tm, D), lambda i: (i, 0)),
        compiler_params=_cparams("parallel"),
        name="rmsnorm",
    )(x, g.reshape(1, D))


def _rms_concat_kernel(x1_ref, x2_ref, g_ref, x_ref, h_ref, *, n1_blocks):
    i = pl.program_id(0)

    def emit(src_ref):
        x = src_ref[...]
        x_ref[...] = x
        h_ref[...] = _rms_body(x, g_ref[...]).astype(h_ref.dtype)

    @pl.when(i < n1_blocks)
    def _():
        emit(x1_ref)

    @pl.when(i >= n1_blocks)
    def _():
        emit(x2_ref)


def rmsnorm_concat(x1, x2, g):
    (n1, D), n2 = x1.shape, x2.shape[0]
    tm = _pick_tile(int(np.gcd(n1, n2)), (128, 64, 32, 16, 8))
    nb1 = n1 // tm
    out = pl.BlockSpec((tm, D), lambda i: (i, 0))
    return pl.pallas_call(
        functools.partial(_rms_concat_kernel, n1_blocks=nb1),
        out_shape=(jax.ShapeDtypeStruct((n1 + n2, D), F32), jax.ShapeDtypeStruct((n1 + n2, D), BF16)),
        grid=((n1 + n2) // tm,),
        in_specs=[pl.BlockSpec((tm, D), lambda i: (jnp.minimum(i, nb1 - 1), 0)),
                  pl.BlockSpec((tm, D), lambda i: (jnp.maximum(i - nb1, 0), 0)),
                  pl.BlockSpec((1, D), lambda i: (0, 0))],
        out_specs=(out, out),
        compiler_params=_cparams("arbitrary"),
        name="rmsnorm_concat",
    )(x1, x2, g.reshape(1, D))


def _pack_pairs(h):
    half = h.shape[1] // 2
    bits = lax.bitcast_convert_type(h.astype(BF16).astype(F32), jnp.uint32)
    return bits[:, :half] | (bits[:, half:] >> 16)


def _unpack_pairs_f32(w):
    hi = lax.bitcast_convert_type(w & jnp.uint32(0xFFFF0000), F32)
    lo = lax.bitcast_convert_type(w << 16, F32)
    return hi, lo


ROUTE_E0, ROUTE_RANK0, ROUTE_GATE0 = 0, 2, 4


def _rms_router_kernel(x_ref, g_ref, wr_ref, br_ref, tri_ref, h_ref, rt_ref, cnt_ref, run_ref):
    @pl.when(pl.program_id(0) == 0)
    def _():
        run_ref[...] = jnp.zeros_like(run_ref)

    h = _rms_body(x_ref[...], g_ref[...])
    h_ref[...] = _pack_pairs(h)
    lg = jnp.dot(h, wr_ref[...], precision=lax.Precision.HIGHEST, preferred_element_type=F32) + br_ref[...]
    lane = lax.broadcasted_iota(jnp.int32, lg.shape, 1)
    lanef = lane.astype(F32)
    neg, far = -3.0e38, float(LANES)

    def first_max(v):
        m = jnp.max(v, axis=-1, keepdims=True)
        return m, jnp.min(jnp.where(v == m, lanef, far), axis=-1, keepdims=True)

    is_grp = lane < N_GROUPS
    gmax, grp = first_max(jnp.where(is_grp, lg, neg))
    p_grp = 1.0 / jnp.sum(jnp.where(is_grp, jnp.exp(lg - gmax), 0.0), axis=-1, keepdims=True)
    egrp = jnp.right_shift(lane - N_GROUPS, EXPERTS_PER_GROUP.bit_length() - 1).astype(F32)
    in_grp = (lane >= N_GROUPS) & (lane < N_GROUPS + N_EXPERTS) & (egrp == grp)
    el = jnp.where(in_grp, lg, neg)
    v1, i1 = first_max(el)
    v2, i2 = first_max(jnp.where(lanef == i1, neg, el))
    e2 = jnp.exp(v2 - v1)
    gate1 = p_grp * (1.0 / (1.0 + e2))
    gate2 = p_grp * (e2 / (1.0 + e2))
    hit1, hit2 = lanef == i1, lanef == i2
    chosen = jnp.where(hit1 | hit2, 1.0, 0.0)
    before = _dot(tri_ref[...], chosen.astype(BF16)) + run_ref[...]
    rank1 = jnp.sum(jnp.where(hit1, before, 0.0), axis=-1, keepdims=True)
    rank2 = jnp.sum(jnp.where(hit2, before, 0.0), axis=-1, keepdims=True)
    run_ref[...] += jnp.sum(chosen, axis=0, keepdims=True)
    cnt_ref[...] = run_ref[...]
    rt = jnp.where(lane == ROUTE_E0, i1 - N_GROUPS, jnp.where(lane == ROUTE_E0 + 1, i2 - N_GROUPS, 0.0))
    rt = jnp.where(lane == ROUTE_RANK0, rank1, jnp.where(lane == ROUTE_RANK0 + 1, rank2, rt))
    rt_ref[...] = jnp.where(lane == ROUTE_GATE0, gate1, jnp.where(lane == ROUTE_GATE0 + 1, gate2, rt))


def rmsnorm_router(x, g, w_router, b_router):
    T, D = x.shape
    tm = _pick_tile(T, (320, 256, 128, 64, 32, 16, 8))
    tri = jnp.asarray(np.tril(np.ones((tm, tm), np.float32), -1), BF16)
    full = lambda shape: pl.BlockSpec(shape, lambda i: (0, 0))
    return pl.pallas_call(
        _rms_router_kernel,
        out_shape=(jax.ShapeDtypeStruct((T, D // 2), jnp.uint32), jax.ShapeDtypeStruct((T, LANES), F32),
                   jax.ShapeDtypeStruct((1, LANES), F32)),
        grid=(T // tm,),
        in_specs=[pl.BlockSpec((tm, D), lambda i: (i, 0)), full((1, D)), full((D, LANES)), full((1, LANES)),
                  full((tm, tm))],
        out_specs=(pl.BlockSpec((tm, D // 2), lambda i: (i, 0)), pl.BlockSpec((tm, LANES), lambda i: (i, 0)),
                   full((1, LANES))),
        scratch_shapes=[pltpu.VMEM((1, LANES), F32)],
        compiler_params=_cparams("arbitrary"),
        name="rmsnorm_router",
    )(x, g.reshape(1, D), w_router, b_router, tri)


def _combine_kernel(x_ref, y0_ref, y1_ref, g_ref, nw_ref, o1_ref, o2_ref, xbuf, *, nc, n_first_blocks):
    D = x_ref.shape[1]
    g = g_ref[...]
    g0, g1 = g[:, 0:1], g[:, 1:2]
    ss = jnp.zeros((x_ref.shape[0], 1), F32)
    for n in range(D // nc):
        cw = slice(n * (nc // 2), (n + 1) * (nc // 2))
        hi0, lo0 = _unpack_pairs_f32(y0_ref[:, cw])
        hi1, lo1 = _unpack_pairs_f32(y1_ref[:, cw])
        ca = slice(n * nc, n * nc + nc // 2)
        cb = slice(n * nc + nc // 2, (n + 1) * nc)
        xa = x_ref[:, ca] + (hi0 * g0 + hi1 * g1)
        xb = x_ref[:, cb] + (lo0 * g0 + lo1 * g1)
        xbuf[:, ca] = xa
        xbuf[:, cb] = xb
        ss = ss + jnp.sum(xa * xa, axis=-1, keepdims=True) + jnp.sum(xb * xb, axis=-1, keepdims=True)
    xn = xbuf[...]
    y = xn * lax.rsqrt(ss * (1.0 / D) + EPS) * nw_ref[...]
    if n_first_blocks is None:
        o1_ref[...] = xn
        o2_ref[...] = y.astype(o2_ref.dtype)
    else:
        i = pl.program_id(0)

        @pl.when(i < n_first_blocks)
        def _():
            o1_ref[...] = y

        @pl.when(i >= n_first_blocks)
        def _():
            o2_ref[...] = y


def moe_combine(x, y0, y1, gate, norm_w, nc, n_first=None):
    T, D = x.shape
    tm = _pick_tile(T if n_first is None else int(np.gcd(n_first, T - n_first)), (128, 64, 32, 16, 8))
    spec = pl.BlockSpec((tm, D), lambda i: (i, 0))
    yspec = pl.BlockSpec((tm, D // 2), lambda i: (i, 0))
    if n_first is None:
        nb1 = None
        out_shape = (jax.ShapeDtypeStruct((T, D), F32), jax.ShapeDtypeStruct((T, D), BF16))
        out_specs = (spec, spec)
    else:
        nb1 = n_first // tm
        out_shape = (jax.ShapeDtypeStruct((n_first, D), F32), jax.ShapeDtypeStruct((T - n_first, D), F32))
        out_specs = (pl.BlockSpec((tm, D), lambda i: (jnp.minimum(i, nb1 - 1), 0)),
                     pl.BlockSpec((tm, D), lambda i: (jnp.maximum(i - nb1, 0), 0)))
    return pl.pallas_call(
        functools.partial(_combine_kernel, nc=nc, n_first_blocks=nb1), out_shape=out_shape, grid=(T // tm,),
        in_specs=[spec, yspec, yspec, pl.BlockSpec((tm, TOP_K), lambda i: (i, 0)),
                  pl.BlockSpec((1, D), lambda i: (0, 0))],
        out_specs=out_specs, scratch_shapes=[pltpu.VMEM((tm, D), F32)],
        compiler_params=_cparams("arbitrary"), name="moe_combine",
    )(x, y0, y1, gate, norm_w.reshape(1, D))


def _mm_kernel(*refs, n_terms, gated, residual):
    a_refs = refs[:n_terms]
    w_refs = refs[n_terms:2 * n_terms]
    pos = 2 * n_terms
    g_refs = refs[pos:pos + n_terms] if gated else ()
    pos += n_terms if gated else 0
    r_ref = refs[pos] if residual else None
    pos += 1 if residual else 0
    o_ref = refs[pos]
    wb_refs = refs[pos + 1:pos + 1 + n_terms]

    @pl.when(pl.program_id(1) == 0)
    def _():
        for w_ref, wb_ref in zip(w_refs, wb_refs):
            wb_ref[...] = w_ref[...].astype(BF16)

    acc = None
    for i in range(n_terms):
        y = _dot(a_refs[i][...], wb_refs[i][...])
        if gated:
            y = y * _sigmoid(g_refs[i][...])
        acc = y if acc is None else acc + y
    if residual:
        acc = acc + r_ref[...]
    o_ref[...] = acc.astype(o_ref.dtype)


def dense(a_list, w_list, layer, out_dtype, gate_src=None, gate_cols=None, residual=None, name="dense",
          tn=MM_TN, tm_max=640):
    n_terms = len(a_list)
    T = a_list[0].shape[0]
    N = w_list[0].shape[2]
    tm = _pick_tile(T, tuple(c for c in (640, 512, 320, 256, 128, 64, 32, 16, 8) if c <= tm_max))
    in_specs, args = [], []
    for a in a_list:
        in_specs.append(pl.BlockSpec((tm, a.shape[1]), lambda n, m: (m, 0)))
        args.append(a)
    for w in w_list:
        in_specs.append(pl.BlockSpec((None, w.shape[1], tn), lambda n, m: (layer, 0, n)))
        args.append(w)
    if gate_src is not None:
        for c0 in gate_cols:
            in_specs.append(pl.BlockSpec((tm, tn), functools.partial(lambda n, m, cb: (m, cb + n), cb=c0 // tn)))
            args.append(gate_src)
    if residual is not None:
        in_specs.append(pl.BlockSpec((tm, tn), lambda n, m: (m, n)))
        args.append(residual)
    kern = functools.partial(_mm_kernel, n_terms=n_terms, gated=gate_src is not None,
                             residual=residual is not None)
    return pl.pallas_call(
        kern,
        out_shape=jax.ShapeDtypeStruct((T, N), out_dtype),
        grid=(N // tn, T // tm),
        in_specs=in_specs,
        out_specs=pl.BlockSpec((tm, tn), lambda n, m: (m, n)),
        scratch_shapes=[pltpu.VMEM((w.shape[1], tn), BF16) for w in w_list],
        compiler_params=_cparams("arbitrary", "arbitrary"),
        name=name,
    )(*args)


def _fill_rows(call_kwargs, n_in, into):
    kw = dict(call_kwargs, input_output_aliases={n_in: 0})
    kw['in_specs'] = list(kw['in_specs']) + [pl.BlockSpec(memory_space=pl.ANY)]

    def skip_alias(kern):
        def wrapped(*refs):
            return kern(*refs[:n_in], *refs[n_in + 1:])
        return wrapped
    return kw, (into,), skip_alias


def _conv_kernel(a_ref, b_ref, st_ref, w_ref, bdw_ref, lng_ref, lnb_ref, c_ref, ns_ref, ubuf, sh, cbuf, *, tl, rc):
    W1 = CONV_WIDTH - 1

    @pl.when(pl.program_id(1) == 0)
    def _():
        ubuf[0:W1, :] = st_ref[...]

    ubuf[W1:W1 + tl, :] = a_ref[...] * _sigmoid(b_ref[...])
    for r in range(1, SUBLANES):
        span = tl + SUBLANES * ((W1 - r) // SUBLANES)
        sh[r - 1, 0:span, :] = ubuf[r:r + span, :]
    dc = ubuf.shape[1]
    lc = LANES

    def chunk(i, carry):
        r0 = pl.multiple_of(i * rc, SUBLANES)
        for c0 in range(0, dc, lc):
            acc = jnp.zeros((rc, lc), F32)
            for j in range(CONV_WIDTH):
                q, r = divmod(j, SUBLANES)
                src = ubuf if r == 0 else sh.at[r - 1]
                acc = acc + src[pl.ds(r0 + SUBLANES * q, rc), c0:c0 + lc] * w_ref[j:j + 1, c0:c0 + lc]
            cbuf[pl.ds(r0, rc), c0:c0 + lc] = acc
        y = cbuf[pl.ds(r0, rc), :] + bdw_ref[...]
        mu = jnp.mean(y, axis=-1, keepdims=True)
        d = y - mu
        var = jnp.mean(d * d, axis=-1, keepdims=True)
        z = d * lax.rsqrt(var + EPS) * lng_ref[...] + lnb_ref[...]
        c_ref[pl.ds(r0, rc), :] = (z * _sigmoid(z)).astype(c_ref.dtype)
        return carry

    lax.fori_loop(0, tl // rc, chunk, 0)
    tail = ubuf[tl:tl + W1, :]
    ns_ref[...] = tail
    ubuf[0:W1, :] = tail


def conv_branch(proj, row0, B, L, state, w_dw, b_dw, ln_g, ln_b, into):
    dc = w_dw.shape[1]
    tl = _pick_tile(L, (512, 256, 128, 64, 32, 16))
    rc = min(tl, 64)
    nl = L // tl
    rb0 = row0 // tl
    kern = functools.partial(_conv_kernel, tl=tl, rc=rc)
    vec = pl.BlockSpec((1, dc), lambda b, l: (0, 0))
    rows = lambda b, l: (rb0 + b * nl + l, 0)
    kw, extra, adapt = _fill_rows(dict(
        out_shape=(jax.ShapeDtypeStruct((proj.shape[0], dc), BF16),
                   jax.ShapeDtypeStruct((B, CONV_WIDTH - 1, dc), F32)),
        grid=(B, nl),
        in_specs=[pl.BlockSpec((tl, dc), rows),
                  pl.BlockSpec((tl, dc), lambda b, l: (rb0 + b * nl + l, 1)),
                  pl.BlockSpec((None, CONV_WIDTH - 1, dc), lambda b, l: (b, 0, 0)),
                  pl.BlockSpec((CONV_WIDTH, dc), lambda b, l: (0, 0)), vec, vec, vec],
        out_specs=(pl.BlockSpec((tl, dc), rows),
                   pl.BlockSpec((None, CONV_WIDTH - 1, dc), lambda b, l: (b, 0, 0))),
        scratch_shapes=[pltpu.VMEM((tl + CONV_PAD, dc), F32),
                        pltpu.VMEM((SUBLANES - 1, tl + CONV_PAD - SUBLANES, dc), F32),
                        pltpu.VMEM((tl, dc), F32)],
        compiler_params=_cparams("arbitrary", "arbitrary"),
        name="conv_branch"), 7, into)
    return pl.pallas_call(adapt(kern), **kw)(
        proj, proj, state, w_dw, b_dw.reshape(1, dc), ln_g.reshape(1, dc), ln_b.reshape(1, dc), *extra)


def _pair_levels(C):
    t = np.arange(C)[:, None]
    s = np.arange(C)[None, :]
    x = np.bitwise_xor(t, s)
    lvl = np.where(x > 0, np.floor(np.log2(np.maximum(x, 1))).astype(np.int32) + 1, 0)
    return np.where(s <= t, lvl, -1).astype(np.int32)


def _group_boundary(b_ref, hs, m, C):
    if 2 * m >= SUBLANES:
        pieces = [jnp.broadcast_to(b_ref[pl.ds(g * 2 * m + m - 1, 1), hs], (2 * m, LANES))
                  for g in range(C // (2 * m))]
        return pieces[0] if len(pieces) == 1 else jnp.concatenate(pieces, axis=0)
    sub = lax.broadcasted_iota(jnp.int32, (SUBLANES, LANES), 0)
    pieces = []
    for v in range(C // SUBLANES):
        piece = jnp.broadcast_to(b_ref[pl.ds(v * SUBLANES + m - 1, 1), hs], (SUBLANES, LANES))
        for j in range(1, SUBLANES // (2 * m)):
            nxt = jnp.broadcast_to(b_ref[pl.ds(v * SUBLANES + j * 2 * m + m - 1, 1), hs], (SUBLANES, LANES))
            piece = jnp.where(sub >= j * 2 * m, nxt, piece)
        pieces.append(piece)
    return jnp.concatenate(pieces, axis=0)


def _hgrn_kernel(f_ref, i_ref, q_ref, g_ref, s0_ref, loglb_ref, log1m_ref, nw_ref, tri_ref, lvl_ref,
                 o_ref, sout_ref, st_ref, lf_ref, b_ref, *, C, precise):
    cidx = pl.program_id(1)
    prec = lax.Precision.HIGHEST if precise else None
    opd = (lambda a: a) if precise else (lambda a: a.astype(BF16))

    @pl.when(cidx == 0)
    def _():
        def init(h, carry):
            st_ref[h] = s0_ref[h].T
            return carry
        lax.fori_loop(0, HGRN_HEADS, init, 0)

    x = f_ref[...]
    log_sig = jnp.minimum(x, 0.0) - jnp.log(1.0 + jnp.exp(-jnp.abs(x)))
    la = loglb_ref[...]
    lc = log1m_ref[...] + log_sig
    lf = jnp.maximum(la, lc) + jnp.log(1.0 + jnp.exp(-jnp.abs(la - lc)))
    lf_ref[...] = lf
    hi = lf.astype(BF16)
    lo = (lf - hi.astype(F32)).astype(BF16)
    b_ref[...] = _dot(tri_ref[...], hi) + _dot(tri_ref[...], lo)

    row = lax.broadcasted_iota(jnp.int32, (C, LANES), 0)

    def head(h, carry):
        hs = pl.ds(pl.multiple_of(h * LANES, LANES), LANES)
        lf_h = lf_ref[:, hs]
        b = b_ref[:, hs]
        q = q_ref[:, hs]
        v = opd(i_ref[:, hs])
        k = 1.0 - jnp.exp(lf_h)
        lvl = lvl_ref[...]
        p = jnp.where(lvl == 0, _dot_nt(opd(q), opd(k), prec), 0.0)
        m, level = 1, 1
        while m < C:
            upper = (row & m) != 0
            if m == 1:
                xpo = jnp.where(upper, lf_h, 0.0)
            else:
                bnd = _group_boundary(b_ref, hs, m, C)
                xpo = jnp.where(upper, b - bnd, bnd - b)
            ex = jnp.exp(xpo)
            pm = _dot_nt(opd(q * ex), opd(k * ex), prec)
            p = jnp.where(lvl == level, pm, p)
            m *= 2
            level += 1
        s_t = st_ref[h]
        o = _dot(opd(p), v, prec) + _dot_nt(opd(q * jnp.exp(b)), opd(s_t), prec)
        b_last = b_ref[pl.ds(C - 1, 1), hs]
        k_dec = opd(k * jnp.exp(b_last - b))
        st_ref[h] = s_t * jnp.exp(b_last) + _dot_tn(v, k_dec, prec)
        ms = jnp.mean(o * o, axis=-1, keepdims=True)
        gv = g_ref[:, hs]
        o_ref[:, hs] = (o * lax.rsqrt(ms + EPS) * nw_ref[:, hs] * (gv * _sigmoid(gv))).astype(o_ref.dtype)
        return carry

    lax.fori_loop(0, HGRN_HEADS, head, 0, unroll=4)

    @pl.when(cidx == pl.num_programs(1) - 1)
    def _():
        def fin(h, carry):
            sout_ref[h] = st_ref[h].T
            return carry
        lax.fori_loop(0, HGRN_HEADS, fin, 0)


def hgrn_branch(proj, row0, B, L, s0, log_lb, log1m_lb, norm_w, col0, precise, into):
    dk = HGRN_HEADS * HGRN_DK
    C = _pick_tile(L, (HGRN_CHUNK, 64, 32, 16))
    nc = L // C
    rb0 = row0 // C
    cb0 = col0 // dk
    tri = jnp.asarray(np.tril(np.ones((C, C), np.float32)), BF16)
    lvl = jnp.asarray(_pair_levels(C))
    kern = functools.partial(_hgrn_kernel, C=C, precise=precise)

    def pspec(j):
        return pl.BlockSpec((C, dk), functools.partial(lambda b, c, j: (rb0 + b * nc + c, cb0 + j), j=j))

    vec = pl.BlockSpec((1, dk), lambda b, c: (0, 0))
    sspec = pl.BlockSpec((None, HGRN_HEADS, HGRN_DK, HGRN_DV), lambda b, c: (b, 0, 0, 0))
    kw, extra, adapt = _fill_rows(dict(
        out_shape=(jax.ShapeDtypeStruct((proj.shape[0], dk), BF16),
                   jax.ShapeDtypeStruct((B, HGRN_HEADS, HGRN_DK, HGRN_DV), F32)),
        grid=(B, nc),
        in_specs=[pspec(0), pspec(1), pspec(2), pspec(3), sspec, vec, vec, vec,
                  pl.BlockSpec((C, C), lambda b, c: (0, 0)), pl.BlockSpec((C, C), lambda b, c: (0, 0))],
        out_specs=(pl.BlockSpec((C, dk), lambda b, c: (rb0 + b * nc + c, 0)), sspec),
        scratch_shapes=[pltpu.VMEM((HGRN_HEADS, HGRN_DV, HGRN_DK), F32), pltpu.VMEM((C, dk), F32),
                        pltpu.VMEM((C, dk), F32)],
        compiler_params=_cparams("arbitrary", "arbitrary"),
        name="hgrn_branch"), 10, into)
    return pl.pallas_call(adapt(kern), **kw)(
        proj, proj, proj, proj, s0, log_lb.reshape(1, dk), log1m_lb.reshape(1, dk), norm_w.reshape(1, dk), tri, lvl,
        *extra)


def _attn_kernel(q_ref, k_ref, v_ref, o_ref, *, scale):
    hd = q_ref.shape[1] // MEM_HEADS
    for h in range(MEM_HEADS):
        cs = slice(h * hd, (h + 1) * hd)
        s = _dot_nt(q_ref[:, cs].astype(BF16), k_ref[:, cs].astype(BF16)) * scale
        s = s - jnp.max(s, axis=-1, keepdims=True)
        e = jnp.exp(s)
        pr = e / jnp.sum(e, axis=-1, keepdims=True)
        o_ref[:, cs] = _dot(pr.astype(BF16), v_ref[:, cs].astype(BF16)).astype(o_ref.dtype)


def attn_branch(proj, row0, B, L, mem_k, mem_v, col0, into):
    n_mem, dm = mem_k.shape[1], mem_k.shape[2]
    tl = _pick_tile(L, (512, 256, 128, 64, 32, 16))
    nl = L // tl
    rb0 = row0 // tl
    cb0 = col0 // dm
    kern = functools.partial(_attn_kernel, scale=float((dm // MEM_HEADS) ** -0.5))
    mspec = pl.BlockSpec((None, n_mem, dm), lambda b, l: (b, 0, 0))
    kw, extra, adapt = _fill_rows(dict(
        out_shape=jax.ShapeDtypeStruct((proj.shape[0], dm), BF16),
        grid=(B, nl),
        in_specs=[pl.BlockSpec((tl, dm), lambda b, l: (rb0 + b * nl + l, cb0)), mspec, mspec],
        out_specs=pl.BlockSpec((tl, dm), lambda b, l: (rb0 + b * nl + l, 0)),
        compiler_params=_cparams("parallel", "parallel"),
        name="attn_branch"), 3, into)
    return pl.pallas_call(adapt(kern), **kw)(proj, mem_k, mem_v, *extra)


def _moe_kernel(be_ref, nr_ref, nv_ref, x_ref, win_ref, wout_ref, o_ref, acc_ref, hm_ref, *, n1, sb):
    b = pl.program_id(0)
    s = pl.program_id(1)
    nr = nr_ref[b]
    R = x_ref.shape[0]
    F, nc = wout_ref.shape
    nsub = (nr + sb - 1) // sb

    @pl.when(s < n1)
    def _():
        shift = ((s % 2) * 16).astype(jnp.uint32)
        for k in range(1, R // sb + 1):
            rows = slice(0, k * sb)

            @pl.when(nsub == k)
            def _():
                xk = lax.bitcast_convert_type((x_ref[rows, :] << shift) & jnp.uint32(0xFFFF0000), F32).astype(BF16)
                part = _dot(xk, win_ref[...].astype(BF16))

                @pl.when(s == 0)
                def _():
                    acc_ref[rows, :] = part

                @pl.when(s > 0)
                def _():
                    acc_ref[rows, :] += part

                @pl.when(s == n1 - 1)
                def _():
                    a = acc_ref[rows, :F]
                    hm_ref[rows, :] = (a * _sigmoid(a) * acc_ref[rows, F:]).astype(BF16)

    @pl.when(s >= n1)
    def _():
        for k in range(0, R // sb + 1):
            @pl.when(nsub == k)
            def _():
                if k > 0:
                    rows = slice(0, k * sb)
                    o_ref[rows, :] = _pack_pairs(_dot(hm_ref[rows, :], wout_ref[...].astype(BF16)))
                if k < R // sb:
                    o_ref[k * sb:, :] = jnp.zeros((R - k * sb, nc // 2), jnp.uint32)


def moe_experts(xs, blk_e, blk_rows, n_valid, w_ein, w_eout, layer):
    P = xs.shape[0]
    D, F2 = w_ein.shape[2], w_ein.shape[3]
    F = w_eout.shape[2]
    R = MOE_ROWS
    nb = P // R
    n1 = 4
    kc = D // n1
    nc = MOE_NC
    n2 = D // nc

    def x_chunk(b, s, be, nr, nv):
        return (jnp.minimum(b, nv[0] - 1), jnp.where(b < nv[0], jnp.minimum(s // 2, n1 // 2 - 1), n1 // 2 - 1))

    def in_chunk(b, s, be, nr, nv):
        sc = jnp.minimum(s, n1 - 1)
        return (layer, be[b], jnp.where(b < nv[0], (sc % 2) * (n1 // 2) + sc // 2, n1 - 1), 0)

    def out_chunk(b, s, be, nr, nv):
        early = s < 2
        e = jnp.where(early, be[jnp.maximum(b - 1, 0)], be[b])
        col = jnp.where(jnp.logical_or(early, b >= nv[0]), n2 - 1, jnp.maximum(s - n1, 0))
        return (layer, e, 0, col)

    def y_chunk(b, s, be, nr, nv):
        return (b, jnp.maximum(s - n1, 0))

    return pl.pallas_call(
        functools.partial(_moe_kernel, n1=n1, sb=MOE_SUB),
        out_shape=jax.ShapeDtypeStruct((P, D // 2), jnp.uint32),
        grid_spec=pltpu.PrefetchScalarGridSpec(
            num_scalar_prefetch=3, grid=(nb, n1 + n2),
            in_specs=[pl.BlockSpec((R, kc), x_chunk),
                      pl.BlockSpec((None, None, kc, F2), in_chunk),
                      pl.BlockSpec((None, None, F, nc), out_chunk)],
            out_specs=pl.BlockSpec((R, nc // 2), y_chunk),
            scratch_shapes=[pltpu.VMEM((R, F2), F32), pltpu.VMEM((R, F), BF16)]),
        compiler_params=_cparams("arbitrary", "arbitrary"),
        name="moe_experts",
    )(blk_e, blk_rows, n_valid, xs, w_ein, w_eout)


def hier_moe(x, norm_w, w_rg, b_rg, w_re, b_re, w_ein, w_eout, layer, next_norm_w, n_first):
    T, D = x.shape
    n_pad = LANES - N_GROUPS - N_EXPERTS
    w_router = jnp.concatenate([w_rg, w_re, jnp.zeros((D, n_pad), F32)], axis=1)
    b_router = jnp.concatenate([b_rg, b_re, jnp.zeros((n_pad,), F32)]).reshape(1, LANES)
    h, route, lane_counts = rmsnorm_router(x, norm_w, w_router, b_router)
    eid = route[:, ROUTE_E0:ROUTE_E0 + TOP_K].astype(jnp.int32)
    rank = route[:, ROUTE_RANK0:ROUTE_RANK0 + TOP_K].astype(jnp.int32)
    gate = route[:, ROUTE_GATE0:ROUTE_GATE0 + TOP_K]
    counts = lane_counts[0, N_GROUPS:N_GROUPS + N_EXPERTS].astype(jnp.int32)
    R = MOE_ROWS
    A = T * TOP_K
    flat_e = eid.reshape(-1)
    padded = (counts + R - 1) // R * R
    pend = jnp.cumsum(padded)
    pstart = pend - padded
    dest = pstart[flat_e] + rank.reshape(-1)
    nb = -(-(A + N_EXPERTS * (R - 1)) // R)
    P = nb * R
    flat_tok = jnp.repeat(jnp.arange(T, dtype=jnp.int32), TOP_K)
    buf_tok = (jnp.arange(P, dtype=jnp.int32) % T).at[dest].set(flat_tok)
    blk = jnp.arange(nb, dtype=jnp.int32)
    n_valid = (pend[-1] // R).astype(jnp.int32)
    blk_raw = jnp.minimum(jnp.sum(blk[:, None] * R >= pend[None, :], axis=1), N_EXPERTS - 1).astype(jnp.int32)
    live = blk < n_valid
    blk_e = jnp.where(live, blk_raw, blk_raw[jnp.maximum(n_valid - 1, 0)])
    blk_rows = jnp.where(live, jnp.clip(counts[blk_raw] - (blk * R - pstart[blk_raw]), 0, R), 0).astype(jnp.int32)
    xs = jnp.take(h, buf_tok, axis=0, mode="clip")
    y = moe_experts(xs, blk_e, blk_rows, n_valid.reshape(1), w_ein, w_eout, layer)
    dest2 = dest.reshape(T, TOP_K)
    y0 = jnp.take(y, dest2[:, 0], axis=0, mode="clip")
    y1 = jnp.take(y, dest2[:, 1], axis=0, mode="clip")
    return moe_combine(x, y0, y1, gate, next_norm_w, MOE_NC, n_first)


def _layer(x, h, groups, log_lb, log1m_lb, p, layer, next_norm_w, n_first):
    dc = p['w_dw'].shape[1]
    dk = HGRN_HEADS * HGRN_DK
    dm = p['w_mem_out'].shape[1]
    D = x.shape[1]
    proj = dense([h], [p['w_in']], layer, F32, name="in_proj", tn=1024, tm_max=320)
    col_hgrn = 2 * dc
    col_mem = col_hgrn + 4 * dk
    col_gate = col_mem + dm
    T = x.shape[0]
    c, o, om = jnp.zeros((T, dc), BF16), jnp.zeros((T, dk), BF16), jnp.zeros((T, dm), BF16)
    conv_states, hgrn_states = [], []
    for row0, B, L, conv_state, s0, mem_k, mem_v in groups:
        c, ns = conv_branch(proj, row0, B, L, conv_state, p['w_dw'], p['b_dw'], p['conv_ln_g'], p['conv_ln_b'], c)
        o, s_new = hgrn_branch(proj, row0, B, L, s0, log_lb, log1m_lb, p['hgrn_norm'], col_hgrn,
                               L <= HGRN_PRECISE_MAX_LEN, o)
        om = attn_branch(proj, row0, B, L, mem_k.reshape(B, -1, dm), mem_v.reshape(B, -1, dm), col_mem, om)
        conv_states.append(ns); hgrn_states.append(s_new)
    merged = dense([c, o, om], [p['w_conv_out'], p['w_hgrn_out'], p['w_mem_out']], layer, BF16,
                   gate_src=proj, gate_cols=[col_gate, col_gate + D, col_gate + 2 * D], name="branch_merge")
    x = dense([merged], [p['w_out']], layer, F32, residual=x, name="out_proj", tn=1024, tm_max=320)
    outs = hier_moe(x, p['norm_ffn'], p['w_router_group'], p['b_router_group'], p['w_router_expert'],
                    p['b_router_expert'], p['w_exp_in'], p['w_exp_out'], layer, next_norm_w, n_first)
    return outs, conv_states, hgrn_states


def kernel(x_prompt, x_sample, state_conv, state_hgrn, cache_mem_k, cache_mem_v, mem_prompt, norm_mix, norm_mem, norm_ffn, norm_final, w_in, w_dw, b_dw, conv_ln_g, conv_ln_b, w_conv_out, hgrn_lb, hgrn_norm, w_hgrn_out, w_mem_kv, w_mem_out, w_out, w_router_group, b_router_group, w_router_expert, b_router_expert, w_exp_in, w_exp_out):
    depth = w_in.shape[0]
    Bp, Lp, D = x_prompt.shape
    Bs, Ls, _ = x_sample.shape
    n_mem = mem_prompt.shape[1]
    dc = w_dw.shape[2]
    dm = w_mem_out.shape[1]
    lb_all = jnp.cumsum(jax.nn.softmax(hgrn_lb.astype(F32), axis=0), axis=0)
    lb_all = lb_all - lb_all[:1]
    log_lb = jnp.log(lb_all)
    log1m_lb = jnp.log1p(-lb_all)
    x, h = rmsnorm_concat(x_prompt.reshape(Bp * Lp, D), x_sample.reshape(Bs * Ls, D), norm_mix[0])
    conv0 = jnp.zeros((Bp, CONV_WIDTH - 1, dc), F32)
    s0 = jnp.zeros((Bp, HGRN_HEADS, HGRN_DK, HGRN_DV), F32)
    mem_flat = mem_prompt.reshape(Bp * n_mem, D)
    conv_p, hgrn_p, mk_p, mv_p, conv_s, hgrn_s = [], [], [], [], [], []
    for l in range(depth):
        p = dict(norm_mix=norm_mix[l], w_in=w_in, w_dw=w_dw[l], b_dw=b_dw[l], conv_ln_g=conv_ln_g[l],
                 conv_ln_b=conv_ln_b[l], w_conv_out=w_conv_out, hgrn_norm=hgrn_norm[l], w_hgrn_out=w_hgrn_out,
                 w_mem_out=w_mem_out, w_out=w_out, norm_ffn=norm_ffn[l], w_router_group=w_router_group[l],
                 b_router_group=b_router_group[l], w_router_expert=w_router_expert[l],
                 b_router_expert=b_router_expert[l], w_exp_in=w_exp_in, w_exp_out=w_exp_out)
        kv = dense([rmsnorm(mem_flat, norm_mem[l], BF16)], [w_mem_kv], l, F32, name="mem_kv")
        mk = kv[:, :dm].reshape(Bp, n_mem, MEM_HEADS, dm // MEM_HEADS)
        mv = kv[:, dm:].reshape(Bp, n_mem, MEM_HEADS, dm // MEM_HEADS)
        groups = [(0, Bp, Lp, conv0, s0, mk, mv),
                  (Bp * Lp, Bs, Ls, state_conv[l], state_hgrn[l], cache_mem_k[l], cache_mem_v[l])]
        last = l == depth - 1
        outs, cstates, hstates = _layer(x, h, groups, log_lb[l], log1m_lb[l], p, l,
                                        norm_final if last else norm_mix[l + 1], Bp * Lp if last else None)
        if not last:
            x, h = outs
        conv_p.append(cstates[0]); hgrn_p.append(hstates[0]); mk_p.append(mk); mv_p.append(mv)
        conv_s.append(cstates[1]); hgrn_s.append(hstates[1])
    y_prompt = outs[0].reshape(Bp, Lp, D)
    y_sample = outs[1].reshape(Bs, Ls, D)
    return (y_prompt, y_sample, jnp.stack(conv_p), jnp.stack(hgrn_p), jnp.stack(mk_p), jnp.stack(mv_p),
            jnp.stack(conv_s), jnp.stack(hgrn_s))
```

```python
import functools

import numpy as np
import jax
import jax.numpy as jnp
from jax import lax
from jax.experimental import pallas as pl
from jax.experimental.pallas import tpu as pltpu

F32 = jnp.float32
BF16 = jnp.bfloat16

EPS = 1e-6
CONV_WIDTH = 31
HGRN_HEADS = 16
HGRN_DK = 128
HGRN_DV = 128
MEM_HEADS = 4
N_GROUPS = 4
EXPERTS_PER_GROUP = 8
N_EXPERTS = N_GROUPS * EXPERTS_PER_GROUP
TOP_K = 2

LANES = 128
SUBLANES = 8
VMEM_LIMIT_BYTES = 56 * 1024 * 1024
MM_TN = 512
MOE_ROWS = 640
MOE_NC = 2048
MOE_SUB = 128
CONV_PAD = 32
HGRN_CHUNK = 128
HGRN_PRECISE_MAX_LEN = 64


def _cparams(*sem):
    return pltpu.CompilerParams(dimension_semantics=sem, vmem_limit_bytes=VMEM_LIMIT_BYTES)


def _pick_tile(n, cands):
    for c in cands:
        if n % c == 0:
            return c
    return n


def _sigmoid(x):
    return 1.0 / (1.0 + jnp.exp(-x))


def _dot(a, b, precision=None):
    return jnp.dot(a, b, preferred_element_type=F32, precision=precision)


def _dot_nt(a, b, precision=None):
    return lax.dot_general(a, b, (((1,), (1,)), ((), ())), preferred_element_type=F32, precision=precision)


def _dot_tn(a, b, precision=None):
    return lax.dot_general(a, b, (((0,), (0,)), ((), ())), preferred_element_type=F32, precision=precision)


def _rms_body(x, g):
    ms = jnp.mean(x * x, axis=-1, keepdims=True)
    return x * lax.rsqrt(ms + EPS) * g


def _rms_kernel(x_ref, g_ref, o_ref):
    o_ref[...] = _rms_body(x_ref[...], g_ref[...]).astype(o_ref.dtype)


def rmsnorm(x, g, out_dtype):
    T, D = x.shape
    tm = _pick_tile(T, (320, 256, 128, 64, 32, 16, 8))
    return pl.pallas_call(
        _rms_kernel,
        out_shape=jax.ShapeDtypeStruct((T, D), out_dtype),
        grid=(T // tm,),
        in_specs=[pl.BlockSpec((tm, D), lambda i: (i, 0)), pl.BlockSpec((1, D), lambda i: (0, 0))],
        out_specs=pl.BlockSpec((tm, D), lambda i: (i, 0)),
        compiler_params=_cparams("parallel"),
        name="rmsnorm",
    )(x, g.reshape(1, D))


def _rms_concat_kernel(x1_ref, x2_ref, g_ref, x_ref, h_ref, *, n1_blocks):
    i = pl.program_id(0)

    def emit(src_ref):
        x = src_ref[...]
        x_ref[...] = x
        h_ref[...] = _rms_body(x, g_ref[...]).astype(h_ref.dtype)

    @pl.when(i < n1_blocks)
    def _():
        emit(x1_ref)

    @pl.when(i >= n1_blocks)
    def _():
        emit(x2_ref)


def rmsnorm_concat(x1, x2, g):
    (n1, D), n2 = x1.shape, x2.shape[0]
    tm = _pick_tile(int(np.gcd(n1, n2)), (128, 64, 32, 16, 8))
    nb1 = n1 // tm
    out = pl.BlockSpec((tm, D), lambda i: (i, 0))
    return pl.pallas_call(
        functools.partial(_rms_concat_kernel, n1_blocks=nb1),
        out_shape=(jax.ShapeDtypeStruct((n1 + n2, D), F32), jax.ShapeDtypeStruct((n1 + n2, D), BF16)),
        grid=((n1 + n2) // tm,),
        in_specs=[pl.BlockSpec((tm, D), lambda i: (jnp.minimum(i, nb1 - 1), 0)),
                  pl.BlockSpec((tm, D), lambda i: (jnp.maximum(i - nb1, 0), 0)),
                  pl.BlockSpec((1, D), lambda i: (0, 0))],
        out_specs=(out, out),
        compiler_params=_cparams("arbitrary"),
        name="rmsnorm_concat",
    )(x1, x2, g.reshape(1, D))


def _pack_pairs(h):
    half = h.shape[1] // 2
    bits = lax.bitcast_convert_type(h.astype(BF16).astype(F32), jnp.uint32)
    return bits[:, :half] | (bits[:, half:] >> 16)


def _unpack_pairs_f32(w):
    hi = lax.bitcast_convert_type(w & jnp.uint32(0xFFFF0000), F32)
    lo = lax.bitcast_convert_type(w << 16, F32)
    return hi, lo


ROUTE_E0, ROUTE_RANK0, ROUTE_GATE0 = 0, 2, 4


def _rms_router_kernel(x_ref, g_ref, wr_ref, br_ref, tri_ref, h_ref, rt_ref, cnt_ref, run_ref):
    @pl.when(pl.program_id(0) == 0)
    def _():
        run_ref[...] = jnp.zeros_like(run_ref)

    h = _rms_body(x_ref[...], g_ref[...])
    h_ref[...] = _pack_pairs(h)
    lg = jnp.dot(h, wr_ref[...], precision=lax.Precision.HIGHEST, preferred_element_type=F32) + br_ref[...]
    lane = lax.broadcasted_iota(jnp.int32, lg.shape, 1)
    lanef = lane.astype(F32)
    neg, far = -3.0e38, float(LANES)

    def first_max(v):
        m = jnp.max(v, axis=-1, keepdims=True)
        return m, jnp.min(jnp.where(v == m, lanef, far), axis=-1, keepdims=True)

    is_grp = lane < N_GROUPS
    gmax, grp = first_max(jnp.where(is_grp, lg, neg))
    p_grp = 1.0 / jnp.sum(jnp.where(is_grp, jnp.exp(lg - gmax), 0.0), axis=-1, keepdims=True)
    egrp = jnp.right_shift(lane - N_GROUPS, EXPERTS_PER_GROUP.bit_length() - 1).astype(F32)
    in_grp = (lane >= N_GROUPS) & (lane < N_GROUPS + N_EXPERTS) & (egrp == grp)
    el = jnp.where(in_grp, lg, neg)
    v1, i1 = first_max(el)
    v2, i2 = first_max(jnp.where(lanef == i1, neg, el))
    e2 = jnp.exp(v2 - v1)
    gate1 = p_grp * (1.0 / (1.0 + e2))
    gate2 = p_grp * (e2 / (1.0 + e2))
    hit1, hit2 = lanef == i1, lanef == i2
    chosen = jnp.where(hit1 | hit2, 1.0, 0.0)
    before = _dot(tri_ref[...], chosen.astype(BF16)) + run_ref[...]
    rank1 = jnp.sum(jnp.where(hit1, before, 0.0), axis=-1, keepdims=True)
    rank2 = jnp.sum(jnp.where(hit2, before, 0.0), axis=-1, keepdims=True)
    run_ref[...] += jnp.sum(chosen, axis=0, keepdims=True)
    cnt_ref[...] = run_ref[...]
    rt = jnp.where(lane == ROUTE_E0, i1 - N_GROUPS, jnp.where(lane == ROUTE_E0 + 1, i2 - N_GROUPS, 0.0))
    rt = jnp.where(lane == ROUTE_RANK0, rank1, jnp.where(lane == ROUTE_RANK0 + 1, rank2, rt))
    rt_ref[...] = jnp.where(lane == ROUTE_GATE0, gate1, jnp.where(lane == ROUTE_GATE0 + 1, gate2, rt))


def rmsnorm_router(x, g, w_router, b_router):
    T, D = x.shape
    tm = _pick_tile(T, (320, 256, 128, 64, 32, 16, 8))
    tri = jnp.asarray(np.tril(np.ones((tm, tm), np.float32), -1), BF16)
    full = lambda shape: pl.BlockSpec(shape, lambda i: (0, 0))
    return pl.pallas_call(
        _rms_router_kernel,
        out_shape=(jax.ShapeDtypeStruct((T, D // 2), jnp.uint32), jax.ShapeDtypeStruct((T, LANES), F32),
                   jax.ShapeDtypeStruct((1, LANES), F32)),
        grid=(T // tm,),
        in_specs=[pl.BlockSpec((tm, D), lambda i: (i, 0)), full((1, D)), full((D, LANES)), full((1, LANES)),
                  full((tm, tm))],
        out_specs=(pl.BlockSpec((tm, D // 2), lambda i: (i, 0)), pl.BlockSpec((tm, LANES), lambda i: (i, 0)),
                   full((1, LANES))),
        scratch_shapes=[pltpu.VMEM((1, LANES), F32)],
        compiler_params=_cparams("arbitrary"),
        name="rmsnorm_router",
    )(x, g.reshape(1, D), w_router, b_router, tri)


def _combine_kernel(x_ref, y0_ref, y1_ref, g_ref, nw_ref, o1_ref, o2_ref, xbuf, *, nc, n_first_blocks):
    D = x_ref.shape[1]
    g = g_ref[...]
    g0, g1 = g[:, 0:1], g[:, 1:2]
    ss = jnp.zeros((x_ref.shape[0], 1), F32)
    for n in range(D // nc):
        cw = slice(n * (nc // 2), (n + 1) * (nc // 2))
        hi0, lo0 = _unpack_pairs_f32(y0_ref[:, cw])
        hi1, lo1 = _unpack_pairs_f32(y1_ref[:, cw])
        ca = slice(n * nc, n * nc + nc // 2)
        cb = slice(n * nc + nc // 2, (n + 1) * nc)
        xa = x_ref[:, ca] + (hi0 * g0 + hi1 * g1)
        xb = x_ref[:, cb] + (lo0 * g0 + lo1 * g1)
        xbuf[:, ca] = xa
        xbuf[:, cb] = xb
        ss = ss + jnp.sum(xa * xa, axis=-1, keepdims=True) + jnp.sum(xb * xb, axis=-1, keepdims=True)
    xn = xbuf[...]
    y = xn * lax.rsqrt(ss * (1.0 / D) + EPS) * nw_ref[...]
    if n_first_blocks is None:
        o1_ref[...] = xn
        o2_ref[...] = y.astype(o2_ref.dtype)
    else:
        i = pl.program_id(0)

        @pl.when(i < n_first_blocks)
        def _():
            o1_ref[...] = y

        @pl.when(i >= n_first_blocks)
        def _():
            o2_ref[...] = y


def moe_combine(x, yg, gate, norm_w, nc, n_first=None):
    T, D = x.shape
    tm = _pick_tile(T if n_first is None else int(np.gcd(n_first, T - n_first)), (128, 64, 32, 16, 8))
    spec = pl.BlockSpec((tm, D), lambda i: (i, 0))
    yspec = pl.BlockSpec((tm, D // 2), lambda i: (i, 0))
    if n_first is None:
        nb1 = None
        out_shape = (jax.ShapeDtypeStruct((T, D), F32), jax.ShapeDtypeStruct((T, D), BF16))
        out_specs = (spec, spec)
    else:
        nb1 = n_first // tm
        out_shape = (jax.ShapeDtypeStruct((n_first, D), F32), jax.ShapeDtypeStruct((T - n_first, D), F32))
        out_specs = (pl.BlockSpec((tm, D), lambda i: (jnp.minimum(i, nb1 - 1), 0)),
                     pl.BlockSpec((tm, D), lambda i: (jnp.maximum(i - nb1, 0), 0)))
    return pl.pallas_call(
        functools.partial(_combine_kernel, nc=nc, n_first_blocks=nb1), out_shape=out_shape, grid=(T // tm,),
        in_specs=[spec, yspec, pl.BlockSpec((tm, D // 2), lambda i: (i + T // tm, 0)),
                  pl.BlockSpec((tm, TOP_K), lambda i: (i, 0)), pl.BlockSpec((1, D), lambda i: (0, 0))],
        out_specs=out_specs, scratch_shapes=[pltpu.VMEM((tm, D), F32)],
        compiler_params=_cparams("arbitrary"), name="moe_combine",
    )(x, yg, yg, gate, norm_w.reshape(1, D))


def _mm_kernel(*refs, n_terms, gated, residual):
    a_refs = refs[:n_terms]
    w_refs = refs[n_terms:2 * n_terms]
    pos = 2 * n_terms
    g_refs = refs[pos:pos + n_terms] if gated else ()
    pos += n_terms if gated else 0
    r_ref = refs[pos] if residual else None
    pos += 1 if residual else 0
    o_ref = refs[pos]
    wb_refs = refs[pos + 1:pos + 1 + n_terms]

    @pl.when(pl.program_id(1) == 0)
    def _():
        for w_ref, wb_ref in zip(w_refs, wb_refs):
            wb_ref[...] = w_ref[...].astype(BF16)

    acc = None
    for i in range(n_terms):
        y = _dot(a_refs[i][...], wb_refs[i][...])
        if gated:
            y = y * _sigmoid(g_refs[i][...])
        acc = y if acc is None else acc + y
    if residual:
        acc = acc + r_ref[...]
    o_ref[...] = acc.astype(o_ref.dtype)


def dense(a_list, w_list, layer, out_dtype, gate_src=None, gate_cols=None, residual=None, name="dense",
          tn=MM_TN, tm_max=640):
    n_terms = len(a_list)
    T = a_list[0].shape[0]
    N = w_list[0].shape[2]
    tm = _pick_tile(T, tuple(c for c in (640, 512, 320, 256, 128, 64, 32, 16, 8) if c <= tm_max))
    in_specs, args = [], []
    for a in a_list:
        in_specs.append(pl.BlockSpec((tm, a.shape[1]), lambda n, m: (m, 0)))
        args.append(a)
    for w in w_list:
        in_specs.append(pl.BlockSpec((None, w.shape[1], tn), lambda n, m: (layer, 0, n)))
        args.append(w)
    if gate_src is not None:
        for c0 in gate_cols:
            in_specs.append(pl.BlockSpec((tm, tn), functools.partial(lambda n, m, cb: (m, cb + n), cb=c0 // tn)))
            args.append(gate_src)
    if residual is not None:
        in_specs.append(pl.BlockSpec((tm, tn), lambda n, m: (m, n)))
        args.append(residual)
    kern = functools.partial(_mm_kernel, n_terms=n_terms, gated=gate_src is not None,
                             residual=residual is not None)
    return pl.pallas_call(
        kern,
        out_shape=jax.ShapeDtypeStruct((T, N), out_dtype),
        grid=(N // tn, T // tm),
        in_specs=in_specs,
        out_specs=pl.BlockSpec((tm, tn), lambda n, m: (m, n)),
        scratch_shapes=[pltpu.VMEM((w.shape[1], tn), BF16) for w in w_list],
        compiler_params=_cparams("arbitrary", "arbitrary"),
        name=name,
    )(*args)


def _fill_rows(call_kwargs, n_in, into):
    kw = dict(call_kwargs, input_output_aliases={n_in: 0})
    kw['in_specs'] = list(kw['in_specs']) + [pl.BlockSpec(memory_space=pl.ANY)]

    def skip_alias(kern):
        def wrapped(*refs):
            return kern(*refs[:n_in], *refs[n_in + 1:])
        return wrapped
    return kw, (into,), skip_alias


def _conv_kernel(a_ref, b_ref, st_ref, w_ref, bdw_ref, lng_ref, lnb_ref, c_ref, ns_ref, ubuf, sh, cbuf, *, tl, rc):
    W1 = CONV_WIDTH - 1

    @pl.when(pl.program_id(1) == 0)
    def _():
        ubuf[0:W1, :] = st_ref[...]

    ubuf[W1:W1 + tl, :] = a_ref[...] * _sigmoid(b_ref[...])
    for r in range(1, SUBLANES):
        span = tl + SUBLANES * ((W1 - r) // SUBLANES)
        sh[r - 1, 0:span, :] = ubuf[r:r + span, :]
    dc = ubuf.shape[1]
    lc = LANES

    def chunk(i, carry):
        r0 = pl.multiple_of(i * rc, SUBLANES)
        for c0 in range(0, dc, lc):
            acc = jnp.zeros((rc, lc), F32)
            for j in range(CONV_WIDTH):
                q, r = divmod(j, SUBLANES)
                src = ubuf if r == 0 else sh.at[r - 1]
                acc = acc + src[pl.ds(r0 + SUBLANES * q, rc), c0:c0 + lc] * w_ref[j:j + 1, c0:c0 + lc]
            cbuf[pl.ds(r0, rc), c0:c0 + lc] = acc
        y = cbuf[pl.ds(r0, rc), :] + bdw_ref[...]
        mu = jnp.mean(y, axis=-1, keepdims=True)
        d = y - mu
        var = jnp.mean(d * d, axis=-1, keepdims=True)
        z = d * lax.rsqrt(var + EPS) * lng_ref[...] + lnb_ref[...]
        c_ref[pl.ds(r0, rc), :] = (z * _sigmoid(z)).astype(c_ref.dtype)
        return carry

    lax.fori_loop(0, tl // rc, chunk, 0)
    tail = ubuf[tl:tl + W1, :]
    ns_ref[...] = tail
    ubuf[0:W1, :] = tail


def conv_branch(proj, row0, B, L, state, w_dw, b_dw, ln_g, ln_b, into):
    dc = w_dw.shape[1]
    tl = _pick_tile(L, (512, 256, 128, 64, 32, 16))
    rc = min(tl, 64)
    nl = L // tl
    rb0 = row0 // tl
    kern = functools.partial(_conv_kernel, tl=tl, rc=rc)
    vec = pl.BlockSpec((1, dc), lambda b, l: (0, 0))
    rows = lambda b, l: (rb0 + b * nl + l, 0)
    kw, extra, adapt = _fill_rows(dict(
        out_shape=(jax.ShapeDtypeStruct((proj.shape[0], dc), BF16),
                   jax.ShapeDtypeStruct((B, CONV_WIDTH - 1, dc), F32)),
        grid=(B, nl),
        in_specs=[pl.BlockSpec((tl, dc), rows),
                  pl.BlockSpec((tl, dc), lambda b, l: (rb0 + b * nl + l, 1)),
                  pl.BlockSpec((None, CONV_WIDTH - 1, dc), lambda b, l: (b, 0, 0)),
                  pl.BlockSpec((CONV_WIDTH, dc), lambda b, l: (0, 0)), vec, vec, vec],
        out_specs=(pl.BlockSpec((tl, dc), rows),
                   pl.BlockSpec((None, CONV_WIDTH - 1, dc), lambda b, l: (b, 0, 0))),
        scratch_shapes=[pltpu.VMEM((tl + CONV_PAD, dc), F32),
                        pltpu.VMEM((SUBLANES - 1, tl + CONV_PAD - SUBLANES, dc), F32),
                        pltpu.VMEM((tl, dc), F32)],
        compiler_params=_cparams("arbitrary", "arbitrary"),
        name="conv_branch"), 7, into)
    return pl.pallas_call(adapt(kern), **kw)(
        proj, proj, state, w_dw, b_dw.reshape(1, dc), ln_g.reshape(1, dc), ln_b.reshape(1, dc), *extra)


def _pair_levels(C):
    t = np.arange(C)[:, None]
    s = np.arange(C)[None, :]
    x = np.bitwise_xor(t, s)
    lvl = np.where(x > 0, np.floor(np.log2(np.maximum(x, 1))).astype(np.int32) + 1, 0)
    return np.where(s <= t, lvl, -1).astype(np.int32)


def _group_boundary(b_ref, hs, m, C):
    if 2 * m >= SUBLANES:
        pieces = [jnp.broadcast_to(b_ref[pl.ds(g * 2 * m + m - 1, 1), hs], (2 * m, LANES))
                  for g in range(C // (2 * m))]
        return pieces[0] if len(pieces) == 1 else jnp.concatenate(pieces, axis=0)
    sub = lax.broadcasted_iota(jnp.int32, (SUBLANES, LANES), 0)
    pieces = []
    for v in range(C // SUBLANES):
        piece = jnp.broadcast_to(b_ref[pl.ds(v * SUBLANES + m - 1, 1), hs], (SUBLANES, LANES))
        for j in range(1, SUBLANES // (2 * m)):
            nxt = jnp.broadcast_to(b_ref[pl.ds(v * SUBLANES + j * 2 * m + m - 1, 1), hs], (SUBLANES, LANES))
            piece = jnp.where(sub >= j * 2 * m, nxt, piece)
        pieces.append(piece)
    return jnp.concatenate(pieces, axis=0)


def _hgrn_kernel(f_ref, i_ref, q_ref, g_ref, s0_ref, loglb_ref, log1m_ref, nw_ref, tri_ref, lvl_ref,
                 o_ref, sout_ref, st_ref, lf_ref, b_ref, *, C, precise):
    cidx = pl.program_id(1)
    prec = lax.Precision.HIGHEST if precise else None
    opd = (lambda a: a) if precise else (lambda a: a.astype(BF16))

    @pl.when(cidx == 0)
    def _():
        def init(h, carry):
            st_ref[h] = s0_ref[h].T
            return carry
        lax.fori_loop(0, HGRN_HEADS, init, 0)

    x = f_ref[...]
    log_sig = jnp.minimum(x, 0.0) - jnp.log(1.0 + jnp.exp(-jnp.abs(x)))
    la = loglb_ref[...]
    lc = log1m_ref[...] + log_sig
    lf = jnp.maximum(la, lc) + jnp.log(1.0 + jnp.exp(-jnp.abs(la - lc)))
    lf_ref[...] = lf
    hi = lf.astype(BF16)
    lo = (lf - hi.astype(F32)).astype(BF16)
    b_ref[...] = _dot(tri_ref[...], hi) + _dot(tri_ref[...], lo)

    row = lax.broadcasted_iota(jnp.int32, (C, LANES), 0)

    def head(h, carry):
        hs = pl.ds(pl.multiple_of(h * LANES, LANES), LANES)
        lf_h = lf_ref[:, hs]
        b = b_ref[:, hs]
        q = q_ref[:, hs]
        v = opd(i_ref[:, hs])
        k = 1.0 - jnp.exp(lf_h)
        lvl = lvl_ref[...]
        p = jnp.where(lvl == 0, _dot_nt(opd(q), opd(k), prec), 0.0)
        m, level = 1, 1
        while m < C:
            upper = (row & m) != 0
            if m == 1:
                xpo = jnp.where(upper, lf_h, 0.0)
            else:
                bnd = _group_boundary(b_ref, hs, m, C)
                xpo = jnp.where(upper, b - bnd, bnd - b)
            ex = jnp.exp(xpo)
            pm = _dot_nt(opd(q * ex), opd(k * ex), prec)
            p = jnp.where(lvl == level, pm, p)
            m *= 2
            level += 1
        s_t = st_ref[h]
        o = _dot(opd(p), v, prec) + _dot_nt(opd(q * jnp.exp(b)), opd(s_t), prec)
        b_last = b_ref[pl.ds(C - 1, 1), hs]
        k_dec = opd(k * jnp.exp(b_last - b))
        st_ref[h] = s_t * jnp.exp(b_last) + _dot_tn(v, k_dec, prec)
        ms = jnp.mean(o * o, axis=-1, keepdims=True)
        gv = g_ref[:, hs]
        o_ref[:, hs] = (o * lax.rsqrt(ms + EPS) * nw_ref[:, hs] * (gv * _sigmoid(gv))).astype(o_ref.dtype)
        return carry

    lax.fori_loop(0, HGRN_HEADS, head, 0, unroll=4)

    @pl.when(cidx == pl.num_programs(1) - 1)
    def _():
        def fin(h, carry):
            sout_ref[h] = st_ref[h].T
            return carry
        lax.fori_loop(0, HGRN_HEADS, fin, 0)


def hgrn_branch(proj, row0, B, L, s0, log_lb, log1m_lb, norm_w, col0, precise, into):
    dk = HGRN_HEADS * HGRN_DK
    C = _pick_tile(L, (HGRN_CHUNK, 64, 32, 16))
    nc = L // C
    rb0 = row0 // C
    cb0 = col0 // dk
    tri = jnp.asarray(np.tril(np.ones((C, C), np.float32)), BF16)
    lvl = jnp.asarray(_pair_levels(C))
    kern = functools.partial(_hgrn_kernel, C=C, precise=precise)

    def pspec(j):
        return pl.BlockSpec((C, dk), functools.partial(lambda b, c, j: (rb0 + b * nc + c, cb0 + j), j=j))

    vec = pl.BlockSpec((1, dk), lambda b, c: (0, 0))
    sspec = pl.BlockSpec((None, HGRN_HEADS, HGRN_DK, HGRN_DV), lambda b, c: (b, 0, 0, 0))
    kw, extra, adapt = _fill_rows(dict(
        out_shape=(jax.ShapeDtypeStruct((proj.shape[0], dk), BF16),
                   jax.ShapeDtypeStruct((B, HGRN_HEADS, HGRN_DK, HGRN_DV), F32)),
        grid=(B, nc),
        in_specs=[pspec(0), pspec(1), pspec(2), pspec(3), sspec, vec, vec, vec,
                  pl.BlockSpec((C, C), lambda b, c: (0, 0)), pl.BlockSpec((C, C), lambda b, c: (0, 0))],
        out_specs=(pl.BlockSpec((C, dk), lambda b, c: (rb0 + b * nc + c, 0)), sspec),
        scratch_shapes=[pltpu.VMEM((HGRN_HEADS, HGRN_DV, HGRN_DK), F32), pltpu.VMEM((C, dk), F32),
                        pltpu.VMEM((C, dk), F32)],
        compiler_params=_cparams("arbitrary", "arbitrary"),
        name="hgrn_branch"), 10, into)
    return pl.pallas_call(adapt(kern), **kw)(
        proj, proj, proj, proj, s0, log_lb.reshape(1, dk), log1m_lb.reshape(1, dk), norm_w.reshape(1, dk), tri, lvl,
        *extra)


def _attn_kernel(q_ref, k_ref, v_ref, o_ref, *, scale):
    hd = q_ref.shape[1] // MEM_HEADS
    for h in range(MEM_HEADS):
        cs = slice(h * hd, (h + 1) * hd)
        s = _dot_nt(q_ref[:, cs].astype(BF16), k_ref[:, cs].astype(BF16)) * scale
        s = s - jnp.max(s, axis=-1, keepdims=True)
        e = jnp.exp(s)
        pr = e / jnp.sum(e, axis=-1, keepdims=True)
        o_ref[:, cs] = _dot(pr.astype(BF16), v_ref[:, cs].astype(BF16)).astype(o_ref.dtype)


def attn_branch(proj, row0, B, L, mem_k, mem_v, col0, into):
    n_mem, dm = mem_k.shape[1], mem_k.shape[2]
    tl = _pick_tile(L, (512, 256, 128, 64, 32, 16))
    nl = L // tl
    rb0 = row0 // tl
    cb0 = col0 // dm
    kern = functools.partial(_attn_kernel, scale=float((dm // MEM_HEADS) ** -0.5))
    mspec = pl.BlockSpec((None, n_mem, dm), lambda b, l: (b, 0, 0))
    kw, extra, adapt = _fill_rows(dict(
        out_shape=jax.ShapeDtypeStruct((proj.shape[0], dm), BF16),
        grid=(B, nl),
        in_specs=[pl.BlockSpec((tl, dm), lambda b, l: (rb0 + b * nl + l, cb0)), mspec, mspec],
        out_specs=pl.BlockSpec((tl, dm), lambda b, l: (rb0 + b * nl + l, 0)),
        compiler_params=_cparams("parallel", "parallel"),
        name="attn_branch"), 3, into)
    return pl.pallas_call(adapt(kern), **kw)(proj, mem_k, mem_v, *extra)


def _moe_kernel(be_ref, nr_ref, nv_ref, x_ref, win_ref, wout_ref, o_ref, acc_ref, hm_ref, *, n1, sb):
    b = pl.program_id(0)
    s = pl.program_id(1)
    nr = nr_ref[b]
    R = x_ref.shape[0]
    F, nc = wout_ref.shape
    nsub = (nr + sb - 1) // sb

    @pl.when(s < n1)
    def _():
        shift = ((s % 2) * 16).astype(jnp.uint32)
        for k in range(1, R // sb + 1):
            rows = slice(0, k * sb)

            @pl.when(nsub == k)
            def _():
                xk = lax.bitcast_convert_type((x_ref[rows, :] << shift) & jnp.uint32(0xFFFF0000), F32).astype(BF16)
                part = _dot(xk, win_ref[...].astype(BF16))

                @pl.when(s == 0)
                def _():
                    acc_ref[rows, :] = part

                @pl.when(s > 0)
                def _():
                    acc_ref[rows, :] += part

                @pl.when(s == n1 - 1)
                def _():
                    a = acc_ref[rows, :F]
                    hm_ref[rows, :] = (a * _sigmoid(a) * acc_ref[rows, F:]).astype(BF16)

    @pl.when(s >= n1)
    def _():
        for k in range(0, R // sb + 1):
            @pl.when(nsub == k)
            def _():
                if k > 0:
                    rows = slice(0, k * sb)
                    o_ref[rows, :] = _pack_pairs(_dot(hm_ref[rows, :], wout_ref[...].astype(BF16)))
                if k < R // sb:
                    o_ref[k * sb:, :] = jnp.zeros((R - k * sb, nc // 2), jnp.uint32)


def moe_experts(xs, blk_e, blk_rows, n_valid, w_ein, w_eout, layer):
    P = xs.shape[0]
    D, F2 = w_ein.shape[2], w_ein.shape[3]
    F = w_eout.shape[2]
    R = MOE_ROWS
    nb = P // R
    n1 = 4
    kc = D // n1
    nc = MOE_NC
    n2 = D // nc

    def x_chunk(b, s, be, nr, nv):
        return (jnp.minimum(b, nv[0] - 1), jnp.where(b < nv[0], jnp.minimum(s // 2, n1 // 2 - 1), n1 // 2 - 1))

    def in_chunk(b, s, be, nr, nv):
        sc = jnp.minimum(s, n1 - 1)
        return (layer, be[b], jnp.where(b < nv[0], (sc % 2) * (n1 // 2) + sc // 2, n1 - 1), 0)

    def out_chunk(b, s, be, nr, nv):
        early = s < 2
        e = jnp.where(early, be[jnp.maximum(b - 1, 0)], be[b])
        col = jnp.where(jnp.logical_or(early, b >= nv[0]), n2 - 1, jnp.maximum(s - n1, 0))
        return (layer, e, 0, col)

    def y_chunk(b, s, be, nr, nv):
        return (b, jnp.maximum(s - n1, 0))

    return pl.pallas_call(
        functools.partial(_moe_kernel, n1=n1, sb=MOE_SUB),
        out_shape=jax.ShapeDtypeStruct((P, D // 2), jnp.uint32),
        grid_spec=pltpu.PrefetchScalarGridSpec(
            num_scalar_prefetch=3, grid=(nb, n1 + n2),
            in_specs=[pl.BlockSpec((R, kc), x_chunk),
                      pl.BlockSpec((None, None, kc, F2), in_chunk),
                      pl.BlockSpec((None, None, F, nc), out_chunk)],
            out_specs=pl.BlockSpec((R, nc // 2), y_chunk),
            scratch_shapes=[pltpu.VMEM((R, F2), F32), pltpu.VMEM((R, F), BF16)]),
        compiler_params=_cparams("arbitrary", "arbitrary"),
        name="moe_experts",
    )(blk_e, blk_rows, n_valid, xs, w_ein, w_eout)


def hier_moe(x, norm_w, w_rg, b_rg, w_re, b_re, w_ein, w_eout, layer, next_norm_w, n_first):
    T, D = x.shape
    n_pad = LANES - N_GROUPS - N_EXPERTS
    w_router = jnp.concatenate([w_rg, w_re, jnp.zeros((D, n_pad), F32)], axis=1)
    b_router = jnp.concatenate([b_rg, b_re, jnp.zeros((n_pad,), F32)]).reshape(1, LANES)
    h, route, lane_counts = rmsnorm_router(x, norm_w, w_router, b_router)
    eid = route[:, ROUTE_E0:ROUTE_E0 + TOP_K].astype(jnp.int32)
    rank = route[:, ROUTE_RANK0:ROUTE_RANK0 + TOP_K].astype(jnp.int32)
    gate = route[:, ROUTE_GATE0:ROUTE_GATE0 + TOP_K]
    counts = lane_counts[0, N_GROUPS:N_GROUPS + N_EXPERTS].astype(jnp.int32)
    R = MOE_ROWS
    A = T * TOP_K
    flat_e = eid.reshape(-1)
    padded = (counts + R - 1) // R * R
    pend = jnp.cumsum(padded)
    pstart = pend - padded
    dest = pstart[flat_e] + rank.reshape(-1)
    nb = -(-(A + N_EXPERTS * (R - 1)) // R)
    P = nb * R
    flat_tok = jnp.repeat(jnp.arange(T, dtype=jnp.int32), TOP_K)
    buf_tok = (jnp.arange(P, dtype=jnp.int32) % T).at[dest].set(flat_tok)
    blk = jnp.arange(nb, dtype=jnp.int32)
    n_valid = (pend[-1] // R).astype(jnp.int32)
    blk_raw = jnp.minimum(jnp.sum(blk[:, None] * R >= pend[None, :], axis=1), N_EXPERTS - 1).astype(jnp.int32)
    live = blk < n_valid
    blk_e = jnp.where(live, blk_raw, blk_raw[jnp.maximum(n_valid - 1, 0)])
    blk_rows = jnp.where(live, jnp.clip(counts[blk_raw] - (blk * R - pstart[blk_raw]), 0, R), 0).astype(jnp.int32)
    xs = jnp.take(h, buf_tok, axis=0, mode="clip")
    y = moe_experts(xs, blk_e, blk_rows, n_valid.reshape(1), w_ein, w_eout, layer)
    dest2 = dest.reshape(T, TOP_K)
    yg = jnp.take(y, jnp.concatenate([dest2[:, 0], dest2[:, 1]]), axis=0, mode="clip")
    return moe_combine(x, yg, gate, next_norm_w, MOE_NC, n_first)


def _layer(x, h, groups, log_lb, log1m_lb, p, layer, next_norm_w, n_first):
    dc = p['w_dw'].shape[1]
    dk = HGRN_HEADS * HGRN_DK
    dm = p['w_mem_out'].shape[1]
    D = x.shape[1]
    proj = dense([h], [p['w_in']], layer, F32, name="in_proj", tn=1024, tm_max=320)
    col_hgrn = 2 * dc
    col_mem = col_hgrn + 4 * dk
    col_gate = col_mem + dm
    T = x.shape[0]
    c, o, om = jnp.zeros((T, dc), BF16), jnp.zeros((T, dk), BF16), jnp.zeros((T, dm), BF16)
    conv_states, hgrn_states = [], []
    for row0, B, L, conv_state, s0, mem_k, mem_v in groups:
        c, ns = conv_branch(proj, row0, B, L, conv_state, p['w_dw'], p['b_dw'], p['conv_ln_g'], p['conv_ln_b'], c)
        o, s_new = hgrn_branch(proj, row0, B, L, s0, log_lb, log1m_lb, p['hgrn_norm'], col_hgrn,
                               L <= HGRN_PRECISE_MAX_LEN, o)
        om = attn_branch(proj, row0, B, L, mem_k.reshape(B, -1, dm), mem_v.reshape(B, -1, dm), col_mem, om)
        conv_states.append(ns); hgrn_states.append(s_new)
    merged = dense([c, o, om], [p['w_conv_out'], p['w_hgrn_out'], p['w_mem_out']], layer, BF16,
                   gate_src=proj, gate_cols=[col_gate, col_gate + D, col_gate + 2 * D], name="branch_merge")
    x = dense([merged], [p['w_out']], layer, F32, residual=x, name="out_proj", tn=1024, tm_max=320)
    outs = hier_moe(x, p['norm_ffn'], p['w_router_group'], p['b_router_group'], p['w_router_expert'],
                    p['b_router_expert'], p['w_exp_in'], p['w_exp_out'], layer, next_norm_w, n_first)
    return outs, conv_states, hgrn_states


def kernel(x_prompt, x_sample, state_conv, state_hgrn, cache_mem_k, cache_mem_v, mem_prompt, norm_mix, norm_mem, norm_ffn, norm_final, w_in, w_dw, b_dw, conv_ln_g, conv_ln_b, w_conv_out, hgrn_lb, hgrn_norm, w_hgrn_out, w_mem_kv, w_mem_out, w_out, w_router_group, b_router_group, w_router_expert, b_router_expert, w_exp_in, w_exp_out):
    depth = w_in.shape[0]
    Bp, Lp, D = x_prompt.shape
    Bs, Ls, _ = x_sample.shape
    n_mem = mem_prompt.shape[1]
    dc = w_dw.shape[2]
    dm = w_mem_out.shape[1]
    lb_all = jnp.cumsum(jax.nn.softmax(hgrn_lb.astype(F32), axis=0), axis=0)
    lb_all = lb_all - lb_all[:1]
    log_lb = jnp.log(lb_all)
    log1m_lb = jnp.log1p(-lb_all)
    x, h = rmsnorm_concat(x_prompt.reshape(Bp * Lp, D), x_sample.reshape(Bs * Ls, D), norm_mix[0])
    conv0 = jnp.zeros((Bp, CONV_WIDTH - 1, dc), F32)
    s0 = jnp.zeros((Bp, HGRN_HEADS, HGRN_DK, HGRN_DV), F32)
    mem_flat = mem_prompt.reshape(Bp * n_mem, D)
    conv_p, hgrn_p, mk_p, mv_p, conv_s, hgrn_s = [], [], [], [], [], []
    for l in range(depth):
        p = dict(norm_mix=norm_mix[l], w_in=w_in, w_dw=w_dw[l], b_dw=b_dw[l], conv_ln_g=conv_ln_g[l],
                 conv_ln_b=conv_ln_b[l], w_conv_out=w_conv_out, hgrn_norm=hgrn_norm[l], w_hgrn_out=w_hgrn_out,
                 w_mem_out=w_mem_out, w_out=w_out, norm_ffn=norm_ffn[l], w_router_group=w_router_group[l],
                 b_router_group=b_router_group[l], w_router_expert=w_router_expert[l],
                 b_router_expert=b_router_expert[l], w_exp_in=w_exp_in, w_exp_out=w_exp_out)
        kv = dense([rmsnorm(mem_flat, norm_mem[l], BF16)], [w_mem_kv], l, F32, name="mem_kv")
        mk = kv[:, :dm].reshape(Bp, n_mem, MEM_HEADS, dm // MEM_HEADS)
        mv = kv[:, dm:].reshape(Bp, n_mem, MEM_HEADS, dm // MEM_HEADS)
        groups = [(0, Bp, Lp, conv0, s0, mk, mv),
                  (Bp * Lp, Bs, Ls, state_conv[l], state_hgrn[l], cache_mem_k[l], cache_mem_v[l])]
        last = l == depth - 1
        outs, cstates, hstates = _layer(x, h, groups, log_lb[l], log1m_lb[l], p, l,
                                        norm_final if last else norm_mix[l + 1], Bp * Lp if last else None)
        if not last:
            x, h = outs
        conv_p.append(cstates[0]); hgrn_p.append(hstates[0]); mk_p.append(mk); mv_p.append(mv)
        conv_s.append(cstates[1]); hgrn_s.append(hstates[1])
    y_prompt = outs[0].reshape(Bp, Lp, D)
    y_sample = outs[1].reshape(Bs, Ls, D)
    return (y_prompt, y_sample, jnp.stack(conv_p), jnp.stack(hgrn_p), jnp.stack(mk_p), jnp.stack(mv_p),
            jnp.stack(conv_s), jnp.stack(hgrn_s))
```
